```python
import math
import jax, jax.numpy as jnp
from jax import lax
import numpy as np

D_MODEL = 2048
BATCH = 2
SEQ = 8192
DEPTH = 1
DEC_BATCH = 4
DEC_SEQ = 8192
PAST_LEN = 128

HEAD_DIM = 128
A_Q_HEADS = 8
A_KV_HEADS = 2
A_GROUP = A_Q_HEADS // A_KV_HEADS
WINDOW = 128
BLOCK = 128
B_HEADS = 8
B_QK_DIM = 64
B_V_DIM = 2 * B_QK_DIM
Q_BLOCK = 128
PEER_HEADS = 8
PEER_KEY_DIM = 256
PEER_HALF = PEER_KEY_DIM // 2
N_KEYS = 128
N_EXPERTS = N_KEYS * N_KEYS
PEER_TOPK = 16
PEER_TOKEN_BLOCK = 128
A_WIDTH = A_Q_HEADS * HEAD_DIM
B_WIDTH = B_HEADS * B_V_DIM
SZ_QA = A_Q_HEADS * HEAD_DIM
SZ_KA = A_KV_HEADS * HEAD_DIM
SZ_VA = A_KV_HEADS * HEAD_DIM
SZ_QB = B_HEADS * 2 * B_QK_DIM
SZ_KB = B_HEADS * 2 * B_QK_DIM
SZ_VB = B_HEADS * B_V_DIM
SZ_G = D_MODEL
IN_COLS = SZ_QA + SZ_KA + SZ_VA + SZ_QB + SZ_KB + SZ_VB + 2 * SZ_G
IN_OFFSETS = (SZ_QA,
              SZ_QA + SZ_KA,
              SZ_QA + SZ_KA + SZ_VA,
              SZ_QA + SZ_KA + SZ_VA + SZ_QB,
              SZ_QA + SZ_KA + SZ_VA + SZ_QB + SZ_KB,
              SZ_QA + SZ_KA + SZ_VA + SZ_QB + SZ_KB + SZ_VB,
              SZ_QA + SZ_KA + SZ_VA + SZ_QB + SZ_KB + SZ_VB + SZ_G)
DN_ALPHA = (2.0 * DEPTH) ** 0.25
DN_BETA = (8.0 * DEPTH) ** -0.25
LN_EPS = 1e-5
RMS_EPS = 1e-5

kernel_name = 'hybrid_bidir_swa_diffattn_peer_encoder'


def _alibi_slopes(n):
    return 2.0 ** (-8.0 * jnp.arange(1, n + 1, dtype=jnp.float32) / n)


def _layer_norm(x, g, b):
    xf = x.astype(jnp.float32)
    mu = jnp.mean(xf, axis=-1, keepdims=True)
    var = jnp.mean(jnp.square(xf - mu), axis=-1, keepdims=True)
    y = (xf - mu) * lax.rsqrt(var + LN_EPS)
    return (y * g.astype(jnp.float32) + b.astype(jnp.float32)).astype(x.dtype)


def _window_attention(q, k, v, sink):
    bn, s_len = q.shape[0], q.shape[1]
    nb = s_len // BLOCK
    qb = q.reshape(bn, nb, BLOCK, A_KV_HEADS, A_GROUP, HEAD_DIM)
    pad = ((0, 0), (BLOCK, BLOCK), (0, 0), (0, 0))
    kp = jnp.pad(k, pad).reshape(bn, nb + 2, BLOCK, A_KV_HEADS, HEAD_DIM)
    vp = jnp.pad(v, pad).reshape(bn, nb + 2, BLOCK, A_KV_HEADS, HEAD_DIM)
    kb = jnp.concatenate([kp[:, :-2], kp[:, 1:-1], kp[:, 2:]], axis=2)
    vb = jnp.concatenate([vp[:, :-2], vp[:, 1:-1], vp[:, 2:]], axis=2)
    s = jnp.einsum('bnqhgd,bnkhd->bnhgqk', qb, kb).astype(jnp.float32) * (HEAD_DIM ** -0.5)
    blk = jnp.arange(nb)[:, None] * BLOCK
    qpos = blk + jnp.arange(BLOCK)[None, :]
    kpos = blk - BLOCK + jnp.arange(3 * BLOCK)[None, :]
    dist = jnp.abs(qpos[:, :, None] - kpos[:, None, :])
    valid = (dist <= WINDOW) & (kpos[:, None, :] >= 0) & (kpos[:, None, :] < s_len)
    slopes = _alibi_slopes(A_Q_HEADS).reshape(A_KV_HEADS, A_GROUP)
    bias = -slopes[None, :, :, None, None] * dist.astype(jnp.float32)[:, None, None]
    s = jnp.where(valid[None, :, None, None], s + bias[None], -jnp.inf)
    sink_l = jnp.broadcast_to(sink.astype(jnp.float32).reshape(1, 1, A_KV_HEADS, A_GROUP, 1, 1),
                              s.shape[:-1] + (1,))
    p = jax.nn.softmax(jnp.concatenate([s, sink_l], axis=-1), axis=-1)[..., :-1]
    o = jnp.einsum('bnhgqk,bnkhd->bnqhgd', p.astype(v.dtype), vb)
    return o.reshape(bn, s_len, A_WIDTH)


def _diff_attention(q, k, v, lam, lam_init, norm_g):
    bn, s_len = q.shape[0], q.shape[1]
    nb = s_len // Q_BLOCK
    slopes = _alibi_slopes(B_HEADS)
    kpos = jnp.arange(s_len)
    q_blocks = jnp.moveaxis(q.reshape(bn, nb, Q_BLOCK, B_HEADS, 2, B_QK_DIM), 1, 0)
    starts = jnp.arange(nb) * Q_BLOCK
    scale = B_QK_DIM ** -0.5

    def block(args):
        qb, start = args
        s = jnp.einsum('bqhmd,bkhmd->bmhqk', qb, k).astype(jnp.float32) * scale
        qpos = start + jnp.arange(Q_BLOCK)
        dist = jnp.abs(qpos[:, None] - kpos[None, :]).astype(jnp.float32)
        s = s - slopes[:, None, None] * dist
        p = jax.nn.softmax(s, axis=-1)
        a = p[:, 0] - lam * p[:, 1]
        return jnp.einsum('bhqk,bkhd->bqhd', a.astype(v.dtype), v)

    o = lax.map(block, (q_blocks, starts))
    o = jnp.moveaxis(o, 0, 1).reshape(bn, s_len, B_HEADS, B_V_DIM)
    of = o.astype(jnp.float32)
    of = of * lax.rsqrt(jnp.mean(jnp.square(of), axis=-1, keepdims=True) + RMS_EPS) * norm_g.astype(jnp.float32)
    return (of * (1.0 - lam_init)).astype(v.dtype).reshape(bn, s_len, B_WIDTH)


def _peer(h, wq, subkeys, u_tab, v_tab):
    bn, s_len, d = h.shape
    hb = h.reshape((bn * s_len) // PEER_TOKEN_BLOCK, PEER_TOKEN_BLOCK, d)

    def block(xb):
        t = xb.shape[0]
        q = (xb @ wq).reshape(t, PEER_HEADS, 2, PEER_HALF)
        s = jnp.einsum('thpd,hpnd->thpn', q, subkeys).astype(jnp.float32)
        sv, si = lax.top_k(s, PEER_TOPK)
        cand = (sv[:, :, 0, :, None] + sv[:, :, 1, None, :]).reshape(t, PEER_HEADS, PEER_TOPK * PEER_TOPK)
        cv, ci = lax.top_k(cand, PEER_TOPK)
        i1 = jnp.take_along_axis(si[:, :, 0], ci // PEER_TOPK, axis=-1)
        i2 = jnp.take_along_axis(si[:, :, 1], ci % PEER_TOPK, axis=-1)
        e = i1 * N_KEYS + i2
        g = jax.nn.softmax(cv, axis=-1)
        u = u_tab[e]
        act = jax.nn.gelu(jnp.einsum('td,thkd->thk', xb, u).astype(jnp.float32), approximate=False)
        w = (g * act).astype(xb.dtype)
        return jnp.einsum('thk,thkd->td', w, v_tab[e])

    return lax.map(block, hb).reshape(bn, s_len, d)


def _trunk(x, c, w_ada, b_ada, w_in, sink_a, lam_q1, lam_k1, lam_q2, lam_k2, diff_norm_g,
           w_up_a, w_up_b, w_o, ln1_g, ln1_b, peer_wq, peer_subkeys, peer_u, peer_v, ln2_g, ln2_b):
    bn, s_len, _ = x.shape
    for layer in range(DEPTH):
        mod = jax.nn.silu(c) @ w_ada[layer] + b_ada[layer]
        sh1, sc1, g1, sh2, sc2, g2 = jnp.split(mod[:, None, :], 6, axis=-1)
        h = x * (1 + sc1) + sh1
        qa, ka, va, qb, kb, vb, ga, gb = jnp.split(h @ w_in[layer], IN_OFFSETS, axis=-1)
        ya = _window_attention(qa.reshape(bn, s_len, A_Q_HEADS, HEAD_DIM),
                               ka.reshape(bn, s_len, A_KV_HEADS, HEAD_DIM),
                               va.reshape(bn, s_len, A_KV_HEADS, HEAD_DIM),
                               sink_a[layer])
        lam_init = 0.8 - 0.6 * math.exp(-0.3 * layer)
        lam = (jnp.exp(jnp.sum(lam_q1[layer].astype(jnp.float32) * lam_k1[layer].astype(jnp.float32)))
               - jnp.exp(jnp.sum(lam_q2[layer].astype(jnp.float32) * lam_k2[layer].astype(jnp.float32)))
               + lam_init)
        yb = _diff_attention(qb.reshape(bn, s_len, B_HEADS, 2, B_QK_DIM),
                             kb.reshape(bn, s_len, B_HEADS, 2, B_QK_DIM),
                             vb.reshape(bn, s_len, B_HEADS, B_V_DIM),
                             lam, lam_init, diff_norm_g[layer])
        merged = jax.nn.sigmoid(ga) * (ya @ w_up_a[layer]) + jax.nn.sigmoid(gb) * (yb @ w_up_b[layer])
        x = _layer_norm(DN_ALPHA * x + g1 * (merged @ w_o[layer]), ln1_g[layer], ln1_b[layer])
        h2 = x * (1 + sc2) + sh2
        y2 = _peer(h2, peer_wq[layer], peer_subkeys[layer], peer_u[layer], peer_v[layer])
        x = _layer_norm(DN_ALPHA * x + g2 * y2, ln2_g[layer], ln2_b[layer])
    return x


def setup_inputs(seed: int = 0) -> dict:
    key = jax.random.key(seed)
    ks = jax.random.split(key, 24)
    f32 = jnp.float32
    L = DEPTH

    def nrm(k, shape, s):
        return jax.random.normal(k, shape, f32) * s

    col_scale = jnp.concatenate([
        jnp.ones((SZ_QA + SZ_KA,), f32), jnp.full((SZ_VA,), DN_BETA, f32),
        jnp.ones((SZ_QB + SZ_KB,), f32), jnp.full((SZ_VB,), DN_BETA, f32),
        jnp.ones((2 * SZ_G,), f32)])
    return {
        'x_prompt': nrm(ks[0], (BATCH, SEQ, D_MODEL), 1.0),
        'x_sample': nrm(ks[1], (DEC_BATCH, DEC_SEQ, D_MODEL), 1.0),
        'c_prompt': nrm(ks[2], (BATCH, D_MODEL), 1.0),
        'c_sample': nrm(ks[3], (DEC_BATCH, D_MODEL), 1.0),
        'w_ada': nrm(ks[4], (L, D_MODEL, 6 * D_MODEL), D_MODEL ** -0.5),
        'b_ada': nrm(ks[5], (L, 6 * D_MODEL), 0.02),
        'w_in': nrm(ks[6], (L, D_MODEL, IN_COLS), D_MODEL ** -0.5) * col_scale,
        'sink_a': nrm(ks[7], (L, A_Q_HEADS), 0.5),
        'lam_q1': nrm(ks[8], (L, B_QK_DIM), 0.1),
        'lam_k1': nrm(ks[9], (L, B_QK_DIM), 0.1),
        'lam_q2': nrm(ks[10], (L, B_QK_DIM), 0.1),
        'lam_k2': nrm(ks[11], (L, B_QK_DIM), 0.1),
        'diff_norm_g': 1.0 + nrm(ks[12], (L, B_V_DIM), 0.02),
        'w_up_a': nrm(ks[13], (L, A_WIDTH, D_MODEL), A_WIDTH ** -0.5),
        'w_up_b': nrm(ks[14], (L, B_WIDTH, D_MODEL), B_WIDTH ** -0.5),
        'w_o': nrm(ks[15], (L, D_MODEL, D_MODEL), DN_BETA * D_MODEL ** -0.5),
        'ln1_g': 1.0 + nrm(ks[16], (L, D_MODEL), 0.02),
        'ln1_b': nrm(ks[17], (L, D_MODEL), 0.02),
        'peer_wq': nrm(ks[18], (L, D_MODEL, PEER_HEADS * PEER_KEY_DIM), D_MODEL ** -0.5),
        'peer_subkeys': nrm(ks[19], (L, PEER_HEADS, 2, N_KEYS, PEER_HALF), PEER_HALF ** -0.5),
        'peer_u': nrm(ks[20], (L, N_EXPERTS, D_MODEL), D_MODEL ** -0.5),
        'peer_v': nrm(ks[21], (L, N_EXPERTS, D_MODEL), DN_BETA * PEER_HEADS ** -0.5),
        'ln2_g': 1.0 + nrm(ks[22], (L, D_MODEL), 0.02),
        'ln2_b': nrm(ks[23], (L, D_MODEL), 0.02),
    }


def reference(x_prompt, x_sample, c_prompt, c_sample, w_ada, b_ada, w_in, sink_a,
              lam_q1, lam_k1, lam_q2, lam_k2, diff_norm_g, w_up_a, w_up_b, w_o,
              ln1_g, ln1_b, peer_wq, peer_subkeys, peer_u, peer_v, ln2_g, ln2_b):
    y_prompt = _trunk(x_prompt, c_prompt, w_ada, b_ada, w_in, sink_a, lam_q1, lam_k1, lam_q2, lam_k2,
                      diff_norm_g, w_up_a, w_up_b, w_o, ln1_g, ln1_b, peer_wq, peer_subkeys,
                      peer_u, peer_v, ln2_g, ln2_b)
    y_sample = _trunk(x_sample, c_sample, w_ada, b_ada, w_in, sink_a, lam_q1, lam_k1, lam_q2, lam_k2,
                      diff_norm_g, w_up_a, w_up_b, w_o, ln1_g, ln1_b, peer_wq, peer_subkeys,
                      peer_u, peer_v, ln2_g, ln2_b)
    return (y_prompt, y_sample)
```

```python
import functools
import math

import jax
import jax.numpy as jnp
from jax import lax
from jax.experimental import pallas as pl
from jax.experimental.pallas import tpu as pltpu

F32 = jnp.float32
BF16 = jnp.bfloat16

D_MODEL = 2048
HEAD_DIM = 128
A_Q_HEADS = 8
A_KV_HEADS = 2
A_GROUP = A_Q_HEADS // A_KV_HEADS
WINDOW = 128
B_HEADS = 8
B_QK_DIM = 64
PEER_HEADS = 8
N_KEYS = 128
N_EXPERTS = N_KEYS * N_KEYS
PEER_TOPK = 16
SZ_QA = A_Q_HEADS * HEAD_DIM
SZ_KA = A_KV_HEADS * HEAD_DIM
SZ_QB = B_HEADS * 2 * B_QK_DIM
SZ_VB = B_HEADS * 2 * B_QK_DIM
DEPTH = 1
DN_ALPHA = (2.0 * DEPTH) ** 0.25
LN_EPS = 1e-5
RMS_EPS = 1e-5
LAM_INIT = 0.8 - 0.6 * math.exp(-0.3 * 0)

VMEM_LIMIT_V7X = 56 * 1024 * 1024

NT_DIMS = (((1,), (1,)), ((), ()))
TN_DIMS = (((0,), (0,)), ((), ()))


def _params(sem):
    return pltpu.CompilerParams(dimension_semantics=sem, vmem_limit_bytes=VMEM_LIMIT_V7X)


def _const_spec(shape):
    nd = len(shape)
    return pl.BlockSpec(shape, lambda *_: (0,) * nd, pipeline_mode=pl.Buffered(1))


def _layer_norm(r, g, b):
    mu = jnp.mean(r, axis=-1, keepdims=True)
    c = r - mu
    var = jnp.mean(c * c, axis=-1, keepdims=True)
    return c * lax.rsqrt(var + LN_EPS) * g + b


def _ada_kernel(c_ref, w_ref, b_ref, o_ref):
    c = c_ref[...]
    a = (c * jax.nn.sigmoid(c)).astype(BF16)
    o_ref[...] = jnp.dot(a, w_ref[...].astype(BF16), preferred_element_type=F32) + b_ref[...]


def _ada(c_all, w_ada, b_ada):
    nb, d = c_all.shape
    n = w_ada.shape[1]
    tn = 1024
    return pl.pallas_call(
        _ada_kernel,
        grid=(n // tn,),
        in_specs=[pl.BlockSpec((nb, d), lambda j: (0, 0)),
                  pl.BlockSpec((d, tn), lambda j: (0, j)),
                  pl.BlockSpec((1, tn), lambda j: (0, j))],
        out_specs=pl.BlockSpec((nb, tn), lambda j: (0, j)),
        out_shape=jax.ShapeDtypeStruct((nb, n), F32),
        compiler_params=_params(("arbitrary",)),
        name="ada",
    )(c_all, w_ada, b_ada.reshape(1, n))


def _proj_kernel(x_ref, sc_ref, sh_ref, w_ref, o_ref, h_scr, *, gate):
    @pl.when(pl.program_id(1) == 0)
    def _():
        h = x_ref[...] * (1.0 + sc_ref[0]) + sh_ref[0]
        h_scr[...] = h.astype(BF16)

    acc = jnp.dot(h_scr[...], w_ref[...], preferred_element_type=F32)
    if gate:
        acc = jax.nn.sigmoid(acc)
    o_ref[...] = acc.astype(o_ref.dtype)


def _proj(x2d, mod3, boff, seq, w, tm, tn, out_dtype, gate):
    t, d = x2d.shape
    n = w.shape[1]
    bpb = seq // tm
    return pl.pallas_call(
        functools.partial(_proj_kernel, gate=gate),
        grid=(t // tm, n // tn),
        in_specs=[pl.BlockSpec((tm, d), lambda i, j: (i, 0)),
                  pl.BlockSpec((1, 1, d), lambda i, j: (boff + i // bpb, 0, 1)),
                  pl.BlockSpec((1, 1, d), lambda i, j: (boff + i // bpb, 0, 0)),
                  pl.BlockSpec((d, tn), lambda i, j: (0, j))],
        out_specs=pl.BlockSpec((tm, tn), lambda i, j: (i, j)),
        out_shape=jax.ShapeDtypeStruct((t, n), out_dtype),
        scratch_shapes=[pltpu.VMEM((tm, d), BF16)],
        compiler_params=_params(("parallel", "arbitrary")),
        name="proj_gate" if gate else "proj_qkv",
    )(x2d, mod3, mod3, w)


def _window_kernel(sink_ref, q_ref, kp_ref, kc_ref, kn_ref, vp_ref, vc_ref, vn_ref, o_ref, *, seq):
    blk = WINDOW
    n = pl.program_id(1)
    krel = lax.broadcasted_iota(jnp.int32, (3 * blk, blk), 0) - blk
    qrel = lax.broadcasted_iota(jnp.int32, (3 * blk, blk), 1)
    dist_i = jnp.abs(krel - qrel)
    kabs = n * blk + krel
    valid = (dist_i <= WINDOW) & (kabs >= 0) & (kabs < seq)
    dist = dist_i.astype(F32)
    scale = HEAD_DIM ** -0.5
    for g in range(A_KV_HEADS):
        cs = slice(g * HEAD_DIM, (g + 1) * HEAD_DIM)
        k = jnp.concatenate([kp_ref[:, cs], kc_ref[:, cs], kn_ref[:, cs]], axis=0)
        v = jnp.concatenate([vp_ref[:, cs], vc_ref[:, cs], vn_ref[:, cs]], axis=0)
        for j in range(A_GROUP):
            h = g * A_GROUP + j
            slope = 2.0 ** (-8.0 * (h + 1) / A_Q_HEADS)
            hs = slice(h * HEAD_DIM, (h + 1) * HEAD_DIM)
            q = q_ref[:, hs]
            s = lax.dot_general(k, q, NT_DIMS, preferred_element_type=F32) * scale
            s = jnp.where(valid, s - slope * dist, -jnp.inf)
            sink = sink_ref[h]
            m = jnp.maximum(jnp.max(s, axis=0, keepdims=True), sink)
            p = jnp.exp(s - m)
            l = jnp.sum(p, axis=0, keepdims=True) + jnp.exp(sink - m)
            o_t = lax.dot_general(v, p.astype(BF16), TN_DIMS, preferred_element_type=F32)
            o_t = o_t * (1.0 / l)
            o_ref[:, hs] = o_t.T.astype(o_ref.dtype)


def _window(qkv, sink, nbatch, seq):
    blk = WINDOW
    nb = seq // blk
    t = nbatch * seq
    kcol = (SZ_QA + SZ_QB + 2 * SZ_VB) // (A_KV_HEADS * HEAD_DIM)
    vcol = kcol + 1

    def row(b, n, off):
        return b * nb + jnp.clip(n + off, 0, nb - 1)

    kv_specs = [pl.BlockSpec((blk, SZ_KA), functools.partial(lambda b, n, col, off: (row(b, n, off), col), col=col, off=off))
                for col in (kcol, vcol) for off in (-1, 0, 1)]
    return pl.pallas_call(
        functools.partial(_window_kernel, seq=seq),
        grid=(nbatch, nb),
        in_specs=[pl.BlockSpec(memory_space=pltpu.SMEM),
                  pl.BlockSpec((blk, SZ_QA), lambda b, n: (b * nb + n, 0))] + kv_specs,
        out_specs=pl.BlockSpec((blk, SZ_QA), lambda b, n: (b * nb + n, 0)),
        out_shape=jax.ShapeDtypeStruct((t, SZ_QA), BF16),
        compiler_params=_params(("parallel", "parallel")),
        name="window_attn",
    )(sink, qkv, qkv, qkv, qkv, qkv, qkv, qkv)


def _diff_kernel(lq1_ref, lk1_ref, lq2_ref, lk2_ref, ng_ref, q_ref, k_ref, v_ref, o_ref,
                 qm_scr, m_scr, l_scr, acc_scr, *, tq, tk):
    qi = pl.program_id(1)
    kv = pl.program_id(2)
    nkv = pl.num_programs(2)

    @pl.when(kv == 0)
    def _():
        m_scr[...] = jnp.full(m_scr.shape, -jnp.inf, F32)
        l_scr[...] = jnp.zeros(l_scr.shape, F32)
        acc_scr[...] = jnp.zeros(acc_scr.shape, F32)
        lane = lax.broadcasted_iota(jnp.int32, (tq, 2 * B_QK_DIM), 1)
        for h in range(B_HEADS):
            q = q_ref[:, h * 128:(h + 1) * 128] * (B_QK_DIM ** -0.5)
            qm_scr[2 * h] = jnp.where(lane < B_QK_DIM, q, 0.0).astype(BF16)
            qm_scr[2 * h + 1] = jnp.where(lane >= B_QK_DIM, q, 0.0).astype(BF16)

    kpos = kv * tk + lax.broadcasted_iota(jnp.int32, (tk, tq), 0)
    qpos = qi * tq + lax.broadcasted_iota(jnp.int32, (tk, tq), 1)
    dist = jnp.abs(kpos - qpos).astype(F32)
    for h in range(B_HEADS):
        slope = 2.0 ** (-8.0 * (h + 1) / B_HEADS)
        bias = slope * dist
        k = k_ref[:, h * 128:(h + 1) * 128]
        v = v_ref[:, h * 128:(h + 1) * 128]
        for mp in range(2):
            hm = 2 * h + mp
            s = lax.dot_general(k, qm_scr[hm], NT_DIMS, preferred_element_type=F32) - bias
            m_old = m_scr[hm:hm + 1, :]
            m_new = jnp.maximum(m_old, jnp.max(s, axis=0, keepdims=True))
            alpha = jnp.exp(m_old - m_new)
            p = jnp.exp(s - m_new)
            l_scr[hm:hm + 1, :] = alpha * l_scr[hm:hm + 1, :] + jnp.sum(p, axis=0, keepdims=True)
            m_scr[hm:hm + 1, :] = m_new
            pv = lax.dot_general(v, p.astype(BF16), TN_DIMS, preferred_element_type=F32)
            acc_scr[hm] = alpha * acc_scr[hm] + pv

    @pl.when(kv == nkv - 1)
    def _():
        lam = (jnp.exp(jnp.sum(lq1_ref[...] * lk1_ref[...], axis=-1, keepdims=True))
               - jnp.exp(jnp.sum(lq2_ref[...] * lk2_ref[...], axis=-1, keepdims=True)) + LAM_INIT)
        for h in range(B_HEADS):
            o0 = acc_scr[2 * h] * (1.0 / l_scr[2 * h:2 * h + 1, :])
            o1 = acc_scr[2 * h + 1] * (1.0 / l_scr[2 * h + 1:2 * h + 2, :])
            o = (o0 - lam * o1).T
            o = o * lax.rsqrt(jnp.mean(o * o, axis=-1, keepdims=True) + RMS_EPS) * ng_ref[...]
            o_ref[:, h * 128:(h + 1) * 128] = (o * (1.0 - LAM_INIT)).astype(o_ref.dtype)


def _diff(qkv, lam_q1, lam_k1, lam_q2, lam_k2, norm_g, nbatch, seq, tq, tk):
    t = nbatch * seq
    nq, nk = seq // tq, seq // tk
    w = SZ_QB
    small = lambda a: a.reshape(1, -1).astype(F32)
    sspec = lambda n: pl.BlockSpec((1, n), lambda b, i, j: (0, 0))
    return pl.pallas_call(
        functools.partial(_diff_kernel, tq=tq, tk=tk),
        grid=(nbatch, nq, nk),
        in_specs=[sspec(B_QK_DIM)] * 4 + [sspec(2 * B_QK_DIM),
                  pl.BlockSpec((tq, w), lambda b, i, j: (b * nq + i, 1)),
                  pl.BlockSpec((tk, w), lambda b, i, j: (b * nk + j, 2)),
                  pl.BlockSpec((tk, w), lambda b, i, j: (b * nk + j, 3))],
        out_specs=pl.BlockSpec((tq, w), lambda b, i, j: (b * nq + i, 0)),
        out_shape=jax.ShapeDtypeStruct((t, w), BF16),
        scratch_shapes=[pltpu.VMEM((2 * B_HEADS, tq, 128), BF16),
                        pltpu.VMEM((2 * B_HEADS, tq), F32),
                        pltpu.VMEM((2 * B_HEADS, tq), F32),
                        pltpu.VMEM((2 * B_HEADS, 128, tq), F32)],
        compiler_params=_params(("parallel", "parallel", "arbitrary")),
        name="diff_attn",
    )(small(lam_q1), small(lam_k1), small(lam_q2), small(lam_k2), small(norm_g), qkv, qkv, qkv)


def _merge_kernel(ya_ref, yb_ref, ga_ref, gb_ref, x_ref, g1_ref, sc2_ref, sh2_ref,
                  wua_ref, wub_ref, wo_ref, lg_ref, lb_ref, x1_ref, h2_ref):
    ua = jnp.dot(ya_ref[...], wua_ref[...], preferred_element_type=F32)
    ub = jnp.dot(yb_ref[...], wub_ref[...], preferred_element_type=F32)
    merged = ga_ref[...] * ua + gb_ref[...] * ub
    z = jnp.dot(merged.astype(BF16), wo_ref[...], preferred_element_type=F32)
    r = DN_ALPHA * x_ref[...] + g1_ref[0] * z
    x1 = _layer_norm(r, lg_ref[...], lb_ref[...])
    x1_ref[...] = x1
    h2_ref[...] = (x1 * (1.0 + sc2_ref[0]) + sh2_ref[0]).astype(BF16)


def _merge(ya, yb, gates, x2d, mod3, boff, seq, wua, wub, wo, ln_g, ln_b, tm):
    t, d = x2d.shape
    bpb = seq // tm
    mspec = lambda c: pl.BlockSpec((1, 1, d), lambda i: (boff + i // bpb, 0, c))
    row = lambda n, c=0: pl.BlockSpec((tm, n), lambda i: (i, c))
    return pl.pallas_call(
        _merge_kernel,
        grid=(t // tm,),
        in_specs=[row(SZ_QA), row(SZ_VB), row(d, 0), row(d, 1), row(d),
                  mspec(2), mspec(4), mspec(3),
                  _const_spec(wua.shape), _const_spec(wub.shape), _const_spec(wo.shape),
                  _const_spec((1, d)), _const_spec((1, d))],
        out_specs=[row(d), row(d)],
        out_shape=[jax.ShapeDtypeStruct((t, d), F32), jax.ShapeDtypeStruct((t, d), BF16)],
        compiler_params=_params(("parallel",)),
        name="merge_ln1",
    )(ya, yb, gates, gates, x2d, mod3, mod3, mod3, wua, wub, wo, ln_g.reshape(1, d), ln_b.reshape(1, d))


def _top16(s):
    n, t = s.shape
    iota = lax.broadcasted_iota(jnp.int32, (n, t), 0)
    rank = jnp.full((n, t), float(PEER_TOPK), F32)
    work = s
    vals = []
    for a in range(PEER_TOPK):
        m = jnp.max(work, axis=0, keepdims=True)
        idx = jnp.min(jnp.where(work == m, iota, n), axis=0, keepdims=True)
        sel = iota == idx
        rank = jnp.where(sel, float(a), rank)
        work = jnp.where(sel, -jnp.inf, work)
        vals.append(m)
    return jnp.concatenate(vals, axis=0), rank


def _peer_q_kernel(h2_ref, wq_ref, sub_ref, a1_ref, lrow_ref, a2_ref, r2_ref, q_scr):
    hd = pl.program_id(1)
    kd = N_KEYS

    @pl.when(hd == 0)
    def _():
        q = jnp.dot(h2_ref[...], wq_ref[...], preferred_element_type=F32).astype(BF16)
        for h in range(PEER_HEADS):
            q_scr[h] = q[:, h * 2 * kd:(h + 1) * 2 * kd]

    q = q_scr[hd]
    s1 = lax.dot_general(sub_ref[0, 0], q[:, :kd], NT_DIMS, preferred_element_type=F32)
    s2 = lax.dot_general(sub_ref[0, 1], q[:, kd:], NT_DIMS, preferred_element_type=F32)
    v1, r1 = _top16(s1)
    v2, r2 = _top16(s2)
    cand = jnp.concatenate([v1[a:a + 1, :] + v2 for a in range(PEER_TOPK)], axis=0)
    cv, crank = _top16(cand)
    chosen = jnp.where(crank < float(PEER_TOPK), 1.0, 0.0)
    z = jnp.sum(jnp.exp(cv - cv[0:1, :]), axis=0, keepdims=True)
    inv_z = 1.0 / z
    lrow = jnp.zeros_like(s1)
    for a in range(PEER_TOPK):
        cnt = jnp.sum(chosen[a * PEER_TOPK:(a + 1) * PEER_TOPK, :], axis=0, keepdims=True)
        lrow = jnp.where(r1 == float(a), cnt, lrow)
    a1_ref[0] = jnp.where(r1 < float(PEER_TOPK), jnp.exp(s1 - v1[0:1, :]) * inv_z, 0.0)
    lrow_ref[0] = lrow
    a2_ref[0] = jnp.exp(s2 - v2[0:1, :])
    r2_ref[0] = r2


def _peer_q(h2, wq, sub, tm):
    t, d = h2.shape
    out = jax.ShapeDtypeStruct((PEER_HEADS, N_KEYS, t), F32)
    ospec = pl.BlockSpec((1, N_KEYS, tm), lambda i, h: (h, 0, i))
    return pl.pallas_call(
        _peer_q_kernel,
        grid=(t // tm, PEER_HEADS),
        in_specs=[pl.BlockSpec((tm, d), lambda i, h: (i, 0)),
                  _const_spec(wq.shape),
                  pl.BlockSpec((1, 2, N_KEYS, N_KEYS), lambda i, h: (h, 0, 0, 0))],
        out_specs=[ospec] * 4,
        out_shape=[out] * 4,
        scratch_shapes=[pltpu.VMEM((PEER_HEADS, tm, 2 * N_KEYS), BF16)],
        compiler_params=_params(("parallel", "arbitrary")),
        name="peer_select",
    )(h2, wq, sub)


def _peer_kernel(h2_ref, a1_ref, lrow_ref, a2_ref, r2_ref, u_ref, v_ref, x1_ref, g2_ref, lg_ref, lb_ref,
                 o_ref, acc_scr, *, rows):
    j = pl.program_id(1)

    @pl.when(j == 0)
    def _():
        acc_scr[...] = jnp.zeros(acc_scr.shape, F32)

    xu = lax.dot_general(u_ref[...], h2_ref[...], NT_DIMS, preferred_element_type=F32)
    act = 0.5 * xu * (1.0 + lax.erf(xu * (2.0 ** -0.5)))
    pieces = []
    for r in range(rows):
        g = None
        for h in range(PEER_HEADS):
            hit = r2_ref[h] < lrow_ref[h, r:r + 1, :]
            term = jnp.where(hit, a1_ref[h, r:r + 1, :] * a2_ref[h], 0.0)
            g = term if g is None else g + term
        pieces.append(g)
    w_t = (act * jnp.concatenate(pieces, axis=0)).astype(BF16)
    acc_scr[...] += lax.dot_general(w_t, v_ref[...], TN_DIMS, preferred_element_type=F32)

    @pl.when(j == pl.num_programs(1) - 1)
    def _():
        r = DN_ALPHA * x1_ref[...] + g2_ref[0] * acc_scr[...]
        o_ref[...] = _layer_norm(r, lg_ref[...], lb_ref[...])


def _peer(h2, sel, u, v, x1, mod3, boff, seq, ln_g, ln_b, tm, te):
    t, d = h2.shape
    rows = te // N_KEYS
    bpb = seq // tm
    a1, lrow, a2, r2 = sel
    full = pl.BlockSpec((PEER_HEADS, N_KEYS, tm), lambda i, j: (0, 0, i))
    part = pl.BlockSpec((PEER_HEADS, rows, tm), lambda i, j: (0, j, i))
    tab = pl.BlockSpec((te, d), lambda i, j: (j, 0))
    tok = pl.BlockSpec((tm, d), lambda i, j: (i, 0))
    return pl.pallas_call(
        functools.partial(_peer_kernel, rows=rows),
        grid=(t // tm, N_EXPERTS // te),
        in_specs=[tok, part, part, full, full, tab, tab, tok,
                  pl.BlockSpec((1, 1, d), lambda i, j: (boff + i // bpb, 0, 5)),
                  _const_spec((1, d)), _const_spec((1, d))],
        out_specs=tok,
        out_shape=jax.ShapeDtypeStruct((t, d), F32),
        scratch_shapes=[pltpu.VMEM((tm, d), F32)],
        compiler_params=_params(("parallel", "arbitrary")),
        name="peer_experts",
    )(h2, a1, lrow, a2, r2, u, v, x1, mod3, ln_g.reshape(1, d), ln_b.reshape(1, d))


def _trunk(x, mod3, boff, w):
    nbatch, seq, d = x.shape
    x2d = x.reshape(nbatch * seq, d)
    qkv = _proj(x2d, mod3, boff, seq, w["w_qkv"], 512, 1536, BF16, False)
    gates = _proj(x2d, mod3, boff, seq, w["w_gate"], 512, 1024, F32, True)
    ya = _window(qkv, w["sink"], nbatch, seq)
    yb = _diff(qkv, w["lam_q1"], w["lam_k1"], w["lam_q2"], w["lam_k2"], w["norm_g"], nbatch, seq, 256, 512)
    x1, h2 = _merge(ya, yb, gates, x2d, mod3, boff, seq, w["w_up_a"], w["w_up_b"], w["w_o"],
                    w["ln1_g"], w["ln1_b"], 256)
    sel = _peer_q(h2, w["peer_wq"], w["peer_sub"], 256)
    y = _peer(h2, sel, w["peer_u"], w["peer_v"], x1, mod3, boff, seq, w["ln2_g"], w["ln2_b"], 256, 1024)
    return y.reshape(nbatch, seq, d)


def kernel(x_prompt, x_sample, c_prompt, c_sample, w_ada, b_ada, w_in, sink_a, lam_q1, lam_k1, lam_q2, lam_k2, diff_norm_g, w_up_a, w_up_b, w_o, ln1_g, ln1_b, peer_wq, peer_subkeys, peer_u, peer_v, ln2_g, ln2_b):
    layer = 0
    nbp, nbs = x_prompt.shape[0], x_sample.shape[0]
    pad = (-(nbp + nbs)) % 8
    c_all = jnp.concatenate([c_prompt, c_sample, jnp.zeros((pad, D_MODEL), F32)], axis=0)
    mod = _ada(c_all, w_ada[layer], b_ada[layer])
    mod3 = mod.reshape(mod.shape[0], 1, 6 * D_MODEL)

    wi = w_in[layer]
    o_ka = SZ_QA
    o_va = o_ka + SZ_KA
    o_qb = o_va + SZ_KA
    o_kb = o_qb + SZ_QB
    o_vb = o_kb + SZ_QB
    o_g = o_vb + SZ_VB
    w_qkv = jnp.concatenate([wi[:, :o_ka], wi[:, o_qb:o_g], wi[:, o_ka:o_qb]], axis=1).astype(BF16)
    weights = dict(
        w_qkv=w_qkv, w_gate=wi[:, o_g:].astype(BF16), sink=sink_a[layer].astype(F32),
        lam_q1=lam_q1[layer], lam_k1=lam_k1[layer], lam_q2=lam_q2[layer], lam_k2=lam_k2[layer],
        norm_g=diff_norm_g[layer],
        w_up_a=w_up_a[layer].astype(BF16), w_up_b=w_up_b[layer].astype(BF16), w_o=w_o[layer].astype(BF16),
        ln1_g=ln1_g[layer], ln1_b=ln1_b[layer],
        peer_wq=peer_wq[layer].astype(BF16), peer_sub=peer_subkeys[layer].astype(BF16),
        peer_u=peer_u[layer].astype(BF16), peer_v=peer_v[layer].astype(BF16),
        ln2_g=ln2_g[layer], ln2_b=ln2_b[layer],
    )
    y_prompt = _trunk(x_prompt, mod3, 0, weights)
    y_sample = _trunk(x_sample, mod3, nbp, weights)
    return (y_prompt, y_sample)
```

```python
import functools
import math

import jax
import jax.numpy as jnp
from jax import lax
from jax.experimental import pallas as pl
from jax.experimental.pallas import tpu as pltpu

F32 = jnp.float32
BF16 = jnp.bfloat16

D_MODEL = 2048
HEAD_DIM = 128
A_Q_HEADS = 8
A_KV_HEADS = 2
A_GROUP = A_Q_HEADS // A_KV_HEADS
WINDOW = 128
B_HEADS = 8
B_QK_DIM = 64
PEER_HEADS = 8
N_KEYS = 128
N_EXPERTS = N_KEYS * N_KEYS
PEER_TOPK = 16
SZ_QA = A_Q_HEADS * HEAD_DIM
SZ_KA = A_KV_HEADS * HEAD_DIM
SZ_QB = B_HEADS * 2 * B_QK_DIM
SZ_VB = B_HEADS * 2 * B_QK_DIM
DEPTH = 1
DN_ALPHA = (2.0 * DEPTH) ** 0.25
LN_EPS = 1e-5
RMS_EPS = 1e-5
LAM_INIT = 0.8 - 0.6 * math.exp(-0.3 * 0)

VMEM_LIMIT_V7X = 56 * 1024 * 1024

NT_DIMS = (((1,), (1,)), ((), ()))
TN_DIMS = (((0,), (0,)), ((), ()))


LANES = 128
GATE_SUB = 32


def _params(sem):
    return pltpu.CompilerParams(dimension_semantics=sem, vmem_limit_bytes=VMEM_LIMIT_V7X)


def _const_spec(shape):
    nd = len(shape)
    return pl.BlockSpec(shape, lambda *_: (0,) * nd, pipeline_mode=pl.Buffered(1))


def _layer_norm(r, g, b):
    mu = jnp.mean(r, axis=-1, keepdims=True)
    c = r - mu
    var = jnp.mean(c * c, axis=-1, keepdims=True)
    return c * lax.rsqrt(var + LN_EPS) * g + b


def _ada_kernel(c_ref, w_ref, b_ref, o_ref):
    c = c_ref[...]
    a = (c * jax.nn.sigmoid(c)).astype(BF16)
    o_ref[...] = jnp.dot(a, w_ref[...].astype(BF16), preferred_element_type=F32) + b_ref[...]


def _ada(c_all, w_ada, b_ada):
    nb, d = c_all.shape
    n = w_ada.shape[1]
    tn = 1024
    return pl.pallas_call(
        _ada_kernel,
        grid=(n // tn,),
        in_specs=[pl.BlockSpec((nb, d), lambda j: (0, 0)),
                  pl.BlockSpec((d, tn), lambda j: (0, j)),
                  pl.BlockSpec((1, tn), lambda j: (0, j))],
        out_specs=pl.BlockSpec((nb, tn), lambda j: (0, j)),
        out_shape=jax.ShapeDtypeStruct((nb, n), F32),
        compiler_params=_params(("arbitrary",)),
        name="ada",
    )(c_all, w_ada, b_ada.reshape(1, n))


def _proj_kernel(x_ref, sc_ref, sh_ref, w_ref, o_ref, h_scr, *, gate):
    @pl.when(pl.program_id(1) == 0)
    def _():
        h = x_ref[...] * (1.0 + sc_ref[0]) + sh_ref[0]
        h_scr[...] = h.astype(BF16)

    acc = jnp.dot(h_scr[...], w_ref[...], preferred_element_type=F32)
    if gate:
        acc = jax.nn.sigmoid(acc)
    o_ref[...] = acc.astype(o_ref.dtype)


def _proj(x2d, mod3, boff, seq, w, tm, tn, out_dtype, gate):
    t, d = x2d.shape
    n = w.shape[1]
    bpb = seq // tm
    return pl.pallas_call(
        functools.partial(_proj_kernel, gate=gate),
        grid=(t // tm, n // tn),
        in_specs=[pl.BlockSpec((tm, d), lambda i, j: (i, 0)),
                  pl.BlockSpec((1, 1, d), lambda i, j: (boff + i // bpb, 0, 1)),
                  pl.BlockSpec((1, 1, d), lambda i, j: (boff + i // bpb, 0, 0)),
                  pl.BlockSpec((d, tn), lambda i, j: (0, j))],
        out_specs=pl.BlockSpec((tm, tn), lambda i, j: (i, j)),
        out_shape=jax.ShapeDtypeStruct((t, n), out_dtype),
        scratch_shapes=[pltpu.VMEM((tm, d), BF16)],
        compiler_params=_params(("parallel", "arbitrary")),
        name="proj_gate" if gate else "proj_qkv",
    )(x2d, mod3, mod3, w)


def _window_kernel(sink_ref, q_ref, kp_ref, kc_ref, kn_ref, vp_ref, vc_ref, vn_ref, o_ref, *, seq):
    blk = WINDOW
    n = pl.program_id(1)
    krel = lax.broadcasted_iota(jnp.int32, (3 * blk, blk), 0) - blk
    qrel = lax.broadcasted_iota(jnp.int32, (3 * blk, blk), 1)
    dist_i = jnp.abs(krel - qrel)
    kabs = n * blk + krel
    valid = (dist_i <= WINDOW) & (kabs >= 0) & (kabs < seq)
    dist = dist_i.astype(F32)
    scale = HEAD_DIM ** -0.5
    for g in range(A_KV_HEADS):
        cs = slice(g * HEAD_DIM, (g + 1) * HEAD_DIM)
        k = jnp.concatenate([kp_ref[:, cs], kc_ref[:, cs], kn_ref[:, cs]], axis=0)
        v = jnp.concatenate([vp_ref[:, cs], vc_ref[:, cs], vn_ref[:, cs]], axis=0)
        for j in range(A_GROUP):
            h = g * A_GROUP + j
            slope = 2.0 ** (-8.0 * (h + 1) / A_Q_HEADS)
            hs = slice(h * HEAD_DIM, (h + 1) * HEAD_DIM)
            q = q_ref[:, hs]
            s = lax.dot_general(k, q, NT_DIMS, preferred_element_type=F32) * scale
            s = jnp.where(valid, s - slope * dist, -jnp.inf)
            sink = sink_ref[h]
            m = jnp.maximum(jnp.max(s, axis=0, keepdims=True), sink)
            p = jnp.exp(s - m)
            l = jnp.sum(p, axis=0, keepdims=True) + jnp.exp(sink - m)
            o_t = lax.dot_general(v, p.astype(BF16), TN_DIMS, preferred_element_type=F32)
            o_t = o_t * (1.0 / l)
            o_ref[:, hs] = o_t.T.astype(o_ref.dtype)


def _window(qkv, sink, nbatch, seq):
    blk = WINDOW
    nb = seq // blk
    t = nbatch * seq
    kcol = (SZ_QA + SZ_QB + 2 * SZ_VB) // (A_KV_HEADS * HEAD_DIM)
    vcol = kcol + 1

    def row(b, n, off):
        return b * nb + jnp.clip(n + off, 0, nb - 1)

    kv_specs = [pl.BlockSpec((blk, SZ_KA), functools.partial(lambda b, n, col, off: (row(b, n, off), col), col=col, off=off))
                for col in (kcol, vcol) for off in (-1, 0, 1)]
    return pl.pallas_call(
        functools.partial(_window_kernel, seq=seq),
        grid=(nbatch, nb),
        in_specs=[pl.BlockSpec(memory_space=pltpu.SMEM),
                  pl.BlockSpec((blk, SZ_QA), lambda b, n: (b * nb + n, 0))] + kv_specs,
        out_specs=pl.BlockSpec((blk, SZ_QA), lambda b, n: (b * nb + n, 0)),
        out_shape=jax.ShapeDtypeStruct((t, SZ_QA), BF16),
        compiler_params=_params(("parallel", "parallel")),
        name="window_attn",
    )(sink, qkv, qkv, qkv, qkv, qkv, qkv, qkv)


DIFF_AUG = 16
DIFF_SPLIT = 256
DIFF_QK_AHEAD = 1
DIFF_PV_BEHIND = 2
DIFF_ROWS = 128
LOG2E = math.log2(math.e)
LOG2E_PARTS = (1.4453125, -0.00262451171875, 7.063150405883789e-06)


def _diff_step(q_bias_rows, group_shift, k_ref, kc_ref, vt_ref, qt_scr, m_scr, acc_scr, s_scr, p_scr, dist, *,
               tq, tk, on_diag):
    nhm = 2 * B_HEADS
    ns, npb = DIFF_QK_AHEAD + 1, DIFF_PV_BEHIND + 1
    ngrp = tk // DIFF_SPLIT
    tiles = DIFF_SPLIT // DIFF_ROWS

    def qk(hm):
        h = hm // 2
        k_aug = jnp.concatenate([k_ref[:, h * 128:(h + 1) * 128], kc_ref[...]], axis=1)
        s_scr[hm % ns] = jnp.dot(k_aug, qt_scr[hm], preferred_element_type=F32)

    def softmax(hm):
        h, slot, pslot = hm // 2, hm % ns, hm % npb
        slope = 2.0 ** (-8.0 * (h + 1) / B_HEADS)
        shifts = [group_shift(h, g) for g in range(ngrp)]
        alphas = []
        for c in range(tq // LANES):
            cs = slice(c * LANES, (c + 1) * LANES)
            m_blk = None
            for g in range(ngrp):
                mrun = None
                for rb in range(tiles):
                    rs = slice(g * DIFF_SPLIT + rb * DIFF_ROWS, g * DIFF_SPLIT + (rb + 1) * DIFF_ROWS)
                    s = s_scr[slot, rs, cs]
                    if on_diag:
                        s = s - (slope * LOG2E) * dist[rs, cs]
                        s_scr[slot, rs, cs] = s
                    mrun = s if mrun is None else jnp.maximum(mrun, s)
                mg = jnp.max(mrun, axis=0, keepdims=True)
                if shifts[g] is not None:
                    mg = mg + shifts[g]
                m_blk = mg if m_blk is None else jnp.maximum(m_blk, mg)
            m_old = m_scr[hm:hm + 1, cs]
            m_new = jnp.maximum(m_old, m_blk)
            alphas.append(jnp.exp2(m_old - m_new))
            for g in range(ngrp):
                m_g = m_new if shifts[g] is None else m_new - shifts[g]
                for rb in range(tiles):
                    rs = slice(g * DIFF_SPLIT + rb * DIFF_ROWS, g * DIFF_SPLIT + (rb + 1) * DIFF_ROWS)
                    p_scr[pslot, rs, cs] = jnp.exp2(s_scr[slot, rs, cs] - m_g).astype(BF16)
            m_scr[hm:hm + 1, cs] = m_new
        return jnp.concatenate(alphas, axis=1)

    def pv(hm, alpha):
        r = jnp.dot(vt_ref[hm // 2], p_scr[hm % npb], preferred_element_type=F32)
        acc_scr[hm] = alpha * acc_scr[hm] + r

    for hm in range(nhm):
        qt_scr[hm, 128:128 + DIFF_AUG, :] = q_bias_rows(hm // 2)
    for hm in range(DIFF_QK_AHEAD):
        qk(hm)
    alpha = {}
    for t in range(nhm + DIFF_PV_BEHIND):
        if t + DIFF_QK_AHEAD < nhm:
            qk(t + DIFF_QK_AHEAD)
        if t < nhm:
            alpha[t] = softmax(t)
        if t >= DIFF_PV_BEHIND:
            pv(t - DIFF_PV_BEHIND, alpha.pop(t - DIFF_PV_BEHIND))


def _diff_kernel(lq1_ref, lk1_ref, lq2_ref, lk2_ref, ng_ref, q_ref, k_ref, kc_ref, vt_ref, o_ref,
                 qt_scr, m_scr, acc_scr, s_scr, p_scr, dist_scr, *, tq, tk):
    qi = pl.program_id(1)
    kv = pl.program_id(2)
    nkv = pl.num_programs(2)

    @pl.when(kv == 0)
    def _():
        m_scr[...] = jnp.full(m_scr.shape, -jnp.inf, F32)
        acc_scr[...] = jnp.zeros(acc_scr.shape, F32)
        row = lax.broadcasted_iota(jnp.int32, (2 * B_QK_DIM, tq), 0)
        zeros = jnp.zeros((128, tq), BF16)
        for h in range(B_HEADS):
            q_t = (q_ref[:, h * 128:(h + 1) * 128].astype(F32) * (B_QK_DIM ** -0.5 * LOG2E)).T
            qt_scr[2 * h, 0:128, :] = jnp.where(row < B_QK_DIM, q_t, 0.0).astype(BF16)
            qt_scr[2 * h + 1, 0:128, :] = jnp.where(row >= B_QK_DIM, q_t, 0.0).astype(BF16)
            qt_scr[2 * h, 128:256, :] = zeros
            qt_scr[2 * h + 1, 128:256, :] = zeros

    q0 = qi * tq
    k0 = kv * tk
    left = k0 + tk <= q0
    right = k0 >= q0 + tq
    args = (k_ref, kc_ref, vt_ref, qt_scr, m_scr, acc_scr, s_scr, p_scr)

    @pl.when(jnp.logical_or(left, right))
    def _():
        sgn = jnp.where(left, 1.0, -1.0).astype(F32)
        off = (q0 - k0).astype(F32)
        rid = lax.broadcasted_iota(jnp.int32, (DIFF_AUG, tq), 0)
        il = (lax.broadcasted_iota(jnp.int32, (DIFF_AUG, tq), 1) - DIFF_SPLIT // 2).astype(F32)

        def rows(h):
            g = sgn * 2.0 ** (-8.0 * (h + 1) / B_HEADS)
            r = jnp.where(rid < 3, -g * il, 0.0)
            for part, c in enumerate(LOG2E_PARTS):
                r = jnp.where(rid == 3 + part, g * c, r)
            return r.astype(BF16)

        def shift(h, grp):
            return -(sgn * 2.0 ** (-8.0 * (h + 1) / B_HEADS) * LOG2E) * (off - float(DIFF_SPLIT * grp))

        _diff_step(rows, shift, *args, None, tq=tq, tk=tk, on_diag=False)

    @pl.when(jnp.logical_not(jnp.logical_or(left, right)))
    def _():
        kpos = k0 + lax.broadcasted_iota(jnp.int32, (tk, tq), 0)
        qpos = q0 + lax.broadcasted_iota(jnp.int32, (tk, tq), 1)
        dist_scr[...] = jnp.abs(kpos - qpos).astype(F32)
        _diff_step(lambda h: jnp.zeros((DIFF_AUG, tq), BF16), lambda h, g: None, *args, dist_scr,
                   tq=tq, tk=tk, on_diag=True)

    @pl.when(kv == nkv - 1)
    def _():
        lam = (jnp.exp(jnp.sum(lq1_ref[...] * lk1_ref[...], axis=-1, keepdims=True))
               - jnp.exp(jnp.sum(lq2_ref[...] * lk2_ref[...], axis=-1, keepdims=True)) + LAM_INIT)
        for h in range(B_HEADS):
            a0, a1 = acc_scr[2 * h], acc_scr[2 * h + 1]
            o0 = a0[0:128] * (1.0 / a0[128:129])
            o1 = a1[0:128] * (1.0 / a1[128:129])
            o = (o0 - lam * o1).T
            o = o * lax.rsqrt(jnp.mean(o * o, axis=-1, keepdims=True) + RMS_EPS) * ng_ref[...]
            o_ref[:, h * 128:(h + 1) * 128] = (o * (1.0 - LAM_INIT)).astype(o_ref.dtype)


def _diff(qkv, vt_aug, lam_q1, lam_k1, lam_q2, lam_k2, norm_g, nbatch, seq, tq, tk):
    t = nbatch * seq
    nq, nk = seq // tq, seq // tk
    w = SZ_QB
    nhm = 2 * B_HEADS
    small = lambda a: a.reshape(1, -1).astype(F32)
    sspec = lambda n: pl.BlockSpec((1, n), lambda b, i, j: (0, 0))
    assert tq == DIFF_SPLIT and tk % DIFF_SPLIT == 0
    jl = (jnp.arange(tk) % DIFF_SPLIT - DIFF_SPLIT // 2).astype(F32)
    kc = jnp.zeros((tk, 128), F32).at[:, 0:3].set(jnp.asarray(LOG2E_PARTS, F32)).at[:, 3:6].set(jl[:, None])
    return pl.pallas_call(
        functools.partial(_diff_kernel, tq=tq, tk=tk),
        grid=(nbatch, nq, nk),
        in_specs=[sspec(B_QK_DIM)] * 4 + [sspec(2 * B_QK_DIM),
                  pl.BlockSpec((tq, w), lambda b, i, j: (b * nq + i, 1)),
                  pl.BlockSpec((tk, w), lambda b, i, j: (b * nk + j, 2)),
                  _const_spec((tk, 128)),
                  pl.BlockSpec((B_HEADS, 128 + DIFF_AUG, tk), lambda b, i, j: (0, 0, b * nk + j))],
        out_specs=pl.BlockSpec((tq, w), lambda b, i, j: (b * nq + i, 0)),
        out_shape=jax.ShapeDtypeStruct((t, w), BF16),
        scratch_shapes=[pltpu.VMEM((nhm, 256, tq), BF16),
                        pltpu.VMEM((nhm, tq), F32),
                        pltpu.VMEM((nhm, 128 + DIFF_AUG, tq), F32),
                        pltpu.VMEM((DIFF_QK_AHEAD + 1, tk, tq), F32),
                        pltpu.VMEM((DIFF_PV_BEHIND + 1, tk, tq), BF16),
                        pltpu.VMEM((tk, tq), F32)],
        compiler_params=_params(("parallel", "parallel", "arbitrary")),
        name="diff_attn",
    )(small(lam_q1), small(lam_k1), small(lam_q2), small(lam_k2), small(norm_g), qkv, qkv, kc.astype(BF16), vt_aug)


def _merge_kernel(ya_ref, yb_ref, ga_ref, gb_ref, x_ref, g1_ref, sc2_ref, sh2_ref,
                  wua_ref, wub_ref, wo_ref, lg_ref, lb_ref, x1_ref, h2_ref):
    ua = jnp.dot(ya_ref[...], wua_ref[...], preferred_element_type=F32)
    ub = jnp.dot(yb_ref[...], wub_ref[...], preferred_element_type=F32)
    merged = ga_ref[...] * ua + gb_ref[...] * ub
    z = jnp.dot(merged.astype(BF16), wo_ref[...], preferred_element_type=F32)
    r = DN_ALPHA * x_ref[...] + g1_ref[0] * z
    x1 = _layer_norm(r, lg_ref[...], lb_ref[...])
    x1_ref[...] = x1
    h2_ref[...] = (x1 * (1.0 + sc2_ref[0]) + sh2_ref[0]).astype(BF16)


def _merge(ya, yb, gates, x2d, mod3, boff, seq, wua, wub, wo, ln_g, ln_b, tm):
    t, d = x2d.shape
    bpb = seq // tm
    mspec = lambda c: pl.BlockSpec((1, 1, d), lambda i: (boff + i // bpb, 0, c))
    row = lambda n, c=0: pl.BlockSpec((tm, n), lambda i: (i, c))
    return pl.pallas_call(
        _merge_kernel,
        grid=(t // tm,),
        in_specs=[row(SZ_QA), row(SZ_VB), row(d, 0), row(d, 1), row(d),
                  mspec(2), mspec(4), mspec(3),
                  _const_spec(wua.shape), _const_spec(wub.shape), _const_spec(wo.shape),
                  _const_spec((1, d)), _const_spec((1, d))],
        out_specs=[row(d), row(d)],
        out_shape=[jax.ShapeDtypeStruct((t, d), F32), jax.ShapeDtypeStruct((t, d), BF16)],
        compiler_params=_params(("parallel",)),
        name="merge_ln1",
    )(ya, yb, gates, gates, x2d, mod3, mod3, mod3, wua, wub, wo, ln_g.reshape(1, d), ln_b.reshape(1, d))


def _top16(s):
    n, t = s.shape
    iota = lax.broadcasted_iota(jnp.int32, (n, t), 0)
    rank = jnp.full((n, t), float(PEER_TOPK), F32)
    work = s
    vals = []
    for a in range(PEER_TOPK):
        m = jnp.max(work, axis=0, keepdims=True)
        idx = jnp.min(jnp.where(work == m, iota, n), axis=0, keepdims=True)
        sel = iota == idx
        rank = jnp.where(sel, float(a), rank)
        work = jnp.where(sel, -jnp.inf, work)
        vals.append(m)
    return jnp.concatenate(vals, axis=0), rank


def _peer_q_kernel(h2_ref, wq_ref, sub_ref, a1_ref, lrow_ref, a2_ref, r2_ref, q_scr):
    hd = pl.program_id(1)
    kd = N_KEYS

    @pl.when(hd == 0)
    def _():
        q = jnp.dot(h2_ref[...], wq_ref[...], preferred_element_type=F32).astype(BF16)
        for h in range(PEER_HEADS):
            q_scr[h] = q[:, h * 2 * kd:(h + 1) * 2 * kd]

    q = q_scr[hd]
    s1 = lax.dot_general(sub_ref[0, 0], q[:, :kd], NT_DIMS, preferred_element_type=F32)
    s2 = lax.dot_general(sub_ref[0, 1], q[:, kd:], NT_DIMS, preferred_element_type=F32)
    v1, r1 = _top16(s1)
    v2, r2 = _top16(s2)
    half = PEER_TOPK // 2
    cand = jnp.concatenate([v1[0:1] + v2] + [v1[a:a + 1] + v2[0:half] for a in range(1, half)]
                           + [v1[half:] + v2[0:1]], axis=0)
    cv, crank = _top16(cand)
    chosen = jnp.where(crank < float(PEER_TOPK), 1.0, 0.0)
    z = jnp.sum(jnp.exp(cv - cv[0:1, :]), axis=0, keepdims=True)
    inv_z = 1.0 / z
    tail = PEER_TOPK + (half - 1) * half
    cnts = [jnp.sum(chosen[0:PEER_TOPK], axis=0, keepdims=True)]
    cnts += [jnp.sum(chosen[PEER_TOPK + (a - 1) * half:PEER_TOPK + a * half], axis=0, keepdims=True)
             for a in range(1, half)]
    cnts += [chosen[tail + a:tail + a + 1] for a in range(half)]
    lrow = jnp.zeros_like(s1)
    for a in range(PEER_TOPK):
        lrow = jnp.where(r1 == float(a), cnts[a], lrow)
    a1_ref[0] = jnp.where(r1 < float(PEER_TOPK), jnp.exp(s1 - v1[0:1, :]) * inv_z, 0.0)
    lrow_ref[0] = lrow
    a2_ref[0] = jnp.exp(s2 - v2[0:1, :])
    r2_ref[0] = r2


def _peer_q(h2, wq, sub, tm):
    t, d = h2.shape
    out = jax.ShapeDtypeStruct((PEER_HEADS, N_KEYS, t), F32)
    ospec = pl.BlockSpec((1, N_KEYS, tm), lambda i, h: (h, 0, i))
    return pl.pallas_call(
        _peer_q_kernel,
        grid=(t // tm, PEER_HEADS),
        in_specs=[pl.BlockSpec((tm, d), lambda i, h: (i, 0)),
                  _const_spec(wq.shape),
                  pl.BlockSpec((1, 2, N_KEYS, N_KEYS), lambda i, h: (h, 0, 0, 0))],
        out_specs=[ospec] * 4,
        out_shape=[out] * 4,
        scratch_shapes=[pltpu.VMEM((PEER_HEADS, tm, 2 * N_KEYS), BF16)],
        compiler_params=_params(("parallel", "arbitrary")),
        name="peer_select",
    )(h2, wq, sub)


def _peer_kernel(h2_ref, a1_ref, lrow_ref, a2_ref, r2_ref, u_ref, v_ref, x1_ref, g2_ref, lg_ref, lb_ref,
                 o_ref, acc_scr, w_scr, xu_scr, g_scr, *, rows):
    j = pl.program_id(1)

    @pl.when(j == 0)
    def _():
        acc_scr[...] = jnp.zeros(acc_scr.shape, F32)
        w_scr[...] = jnp.zeros(w_scr.shape, BF16)

    acc_scr[...] += jnp.dot(w_scr[...], v_ref[...], preferred_element_type=F32)

    tm = h2_ref.shape[0]
    for tc in range(tm // LANES):
        cs = slice(tc * LANES, (tc + 1) * LANES)
        for sb in range(N_KEYS // GATE_SUB):
            ss = slice(sb * GATE_SUB, (sb + 1) * GATE_SUB)
            g = [None] * rows
            for h in range(PEER_HEADS):
                r2 = r2_ref[h, ss, cs]
                a2 = a2_ref[h, ss, cs]
                for r in range(rows):
                    term = jnp.where(r2 < lrow_ref[h, r:r + 1, cs], a1_ref[h, r:r + 1, cs] * a2, 0.0)
                    g[r] = term if g[r] is None else g[r] + term
            for r in range(rows):
                g_scr[r * N_KEYS + sb * GATE_SUB:r * N_KEYS + (sb + 1) * GATE_SUB, cs] = g[r]

    xu_scr[...] = lax.dot_general(u_ref[...], h2_ref[...], NT_DIMS, preferred_element_type=F32)
    for r in range(rows):
        rs = slice(r * N_KEYS, (r + 1) * N_KEYS)
        for tc in range(tm // LANES):
            cs = slice(tc * LANES, (tc + 1) * LANES)
            xu = xu_scr[rs, cs]
            w_t = 0.5 * xu * (1.0 + lax.erf(xu * (2.0 ** -0.5))) * g_scr[rs, cs]
            w_scr[cs, rs] = w_t.T.astype(BF16)

    @pl.when(j == pl.num_programs(1) - 1)
    def _():
        r = DN_ALPHA * x1_ref[...] + g2_ref[0] * acc_scr[...]
        o_ref[...] = _layer_norm(r, lg_ref[...], lb_ref[...])


def _peer(h2, sel, u, v, x1, mod3, boff, seq, ln_g, ln_b, tm, te):
    t, d = h2.shape
    rows = te // N_KEYS
    bpb = seq // tm
    nj = N_EXPERTS // te
    a1, lrow, a2, r2 = sel
    once = dict(pipeline_mode=pl.Buffered(1))
    full = pl.BlockSpec((PEER_HEADS, N_KEYS, tm), lambda i, j: (0, 0, i), **once)
    part = pl.BlockSpec((PEER_HEADS, rows, tm), lambda i, j: (0, jnp.minimum(j, nj - 1), i))
    tok = lambda **kw: pl.BlockSpec((tm, d), lambda i, j: (i, 0), **kw)
    return pl.pallas_call(
        functools.partial(_peer_kernel, rows=rows),
        grid=(t // tm, nj + 1),
        in_specs=[tok(**once), part, part, full, full,
                  pl.BlockSpec((te, d), lambda i, j: (jnp.minimum(j, nj - 1), 0)),
                  pl.BlockSpec((te, d), lambda i, j: (jnp.maximum(j - 1, 0), 0)),
                  tok(**once),
                  pl.BlockSpec((1, 1, d), lambda i, j: (boff + i // bpb, 0, 5)),
                  _const_spec((1, d)), _const_spec((1, d))],
        out_specs=tok(),
        out_shape=jax.ShapeDtypeStruct((t, d), F32),
        scratch_shapes=[pltpu.VMEM((tm, d), F32), pltpu.VMEM((tm, te), BF16),
                        pltpu.VMEM((te, tm), F32), pltpu.VMEM((te, tm), F32)],
        compiler_params=_params(("parallel", "arbitrary")),
        name="peer_experts",
    )(h2, a1, lrow, a2, r2, u, v, x1, mod3, ln_g.reshape(1, d), ln_b.reshape(1, d))


def _trunk(x, mod3, boff, w):
    nbatch, seq, d = x.shape
    x2d = x.reshape(nbatch * seq, d)
    qkv = _proj(x2d, mod3, boff, seq, w["w_qkv"], 512, 1536, BF16, False)
    gates = _proj(x2d, mod3, boff, seq, w["w_gate"], 512, 1024, F32, True)
    ya = _window(qkv, w["sink"], nbatch, seq)
    t = nbatch * seq
    vt = qkv[:, SZ_QA + 2 * SZ_QB:SZ_QA + 2 * SZ_QB + SZ_VB].T.reshape(B_HEADS, 128, t)
    vt_aug = jnp.concatenate([vt, jnp.ones((B_HEADS, DIFF_AUG, t), BF16)], axis=1)
    yb = _diff(qkv, vt_aug, w["lam_q1"], w["lam_k1"], w["lam_q2"], w["lam_k2"], w["norm_g"], nbatch, seq, 256, 1024)
    x1, h2 = _merge(ya, yb, gates, x2d, mod3, boff, seq, w["w_up_a"], w["w_up_b"], w["w_o"],
                    w["ln1_g"], w["ln1_b"], 256)
    sel = _peer_q(h2, w["peer_wq"], w["peer_sub"], 512)
    y = _peer(h2, sel, w["peer_u"], w["peer_v"], x1, mod3, boff, seq, w["ln2_g"], w["ln2_b"], 512, 1024)
    return y.reshape(nbatch, seq, d)


def kernel(x_prompt, x_sample, c_prompt, c_sample, w_ada, b_ada, w_in, sink_a, lam_q1, lam_k1, lam_q2, lam_k2, diff_norm_g, w_up_a, w_up_b, w_o, ln1_g, ln1_b, peer_wq, peer_subkeys, peer_u, peer_v, ln2_g, ln2_b):
    layer = 0
    nbp, nbs = x_prompt.shape[0], x_sample.shape[0]
    pad = (-(nbp + nbs)) % 8
    c_all = jnp.concatenate([c_prompt, c_sample, jnp.zeros((pad, D_MODEL), F32)], axis=0)
    mod = _ada(c_all, w_ada[layer], b_ada[layer])
    mod3 = mod.reshape(mod.shape[0], 1, 6 * D_MODEL)

    wi = w_in[layer]
    o_ka = SZ_QA
    o_va = o_ka + SZ_KA
    o_qb = o_va + SZ_KA
    o_kb = o_qb + SZ_QB
    o_vb = o_kb + SZ_QB
    o_g = o_vb + SZ_VB
    w_qkv = jnp.concatenate([wi[:, :o_ka], wi[:, o_qb:o_g], wi[:, o_ka:o_qb]], axis=1).astype(BF16)
    weights = dict(
        w_qkv=w_qkv, w_gate=wi[:, o_g:].astype(BF16), sink=sink_a[layer].astype(F32),
        lam_q1=lam_q1[layer], lam_k1=lam_k1[layer], lam_q2=lam_q2[layer], lam_k2=lam_k2[layer],
        norm_g=diff_norm_g[layer],
        w_up_a=w_up_a[layer].astype(BF16), w_up_b=w_up_b[layer].astype(BF16), w_o=w_o[layer].astype(BF16),
        ln1_g=ln1_g[layer], ln1_b=ln1_b[layer],
        peer_wq=peer_wq[layer].astype(BF16), peer_sub=peer_subkeys[layer].astype(BF16),
        peer_u=peer_u[layer].astype(BF16), peer_v=peer_v[layer].astype(BF16),
        ln2_g=ln2_g[layer], ln2_b=ln2_b[layer],
    )
    y_prompt = _trunk(x_prompt, mod3, 0, weights)
    y_sample = _trunk(x_sample, mod3, nbp, weights)
    return (y_prompt, y_sample)
```

```python
import functools
import math

import jax
import jax.numpy as jnp
from jax import lax
from jax.experimental import pallas as pl
from jax.experimental.pallas import tpu as pltpu

F32 = jnp.float32
BF16 = jnp.bfloat16

D_MODEL = 2048
HEAD_DIM = 128
A_Q_HEADS = 8
A_KV_HEADS = 2
A_GROUP = A_Q_HEADS // A_KV_HEADS
WINDOW = 128
B_HEADS = 8
B_QK_DIM = 64
PEER_HEADS = 8
N_KEYS = 128
N_EXPERTS = N_KEYS * N_KEYS
PEER_TOPK = 16
SZ_QA = A_Q_HEADS * HEAD_DIM
SZ_KA = A_KV_HEADS * HEAD_DIM
SZ_QB = B_HEADS * 2 * B_QK_DIM
SZ_VB = B_HEADS * 2 * B_QK_DIM
DEPTH = 1
DN_ALPHA = (2.0 * DEPTH) ** 0.25
LN_EPS = 1e-5
RMS_EPS = 1e-5
LAM_INIT = 0.8 - 0.6 * math.exp(-0.3 * 0)

VMEM_LIMIT_V7X = 56 * 1024 * 1024

NT_DIMS = (((1,), (1,)), ((), ()))
TN_DIMS = (((0,), (0,)), ((), ()))


LANES = 128
GATE_SUB = 32


def _params(sem):
    return pltpu.CompilerParams(dimension_semantics=sem, vmem_limit_bytes=VMEM_LIMIT_V7X)


def _const_spec(shape):
    nd = len(shape)
    return pl.BlockSpec(shape, lambda *_: (0,) * nd, pipeline_mode=pl.Buffered(1))


def _layer_norm(r, g, b):
    mu = jnp.mean(r, axis=-1, keepdims=True)
    c = r - mu
    var = jnp.mean(c * c, axis=-1, keepdims=True)
    return c * lax.rsqrt(var + LN_EPS) * g + b


def _ada_kernel(c_ref, w_ref, b_ref, o_ref):
    c = c_ref[...]
    a = (c * jax.nn.sigmoid(c)).astype(BF16)
    o_ref[...] = jnp.dot(a, w_ref[...].astype(BF16), preferred_element_type=F32) + b_ref[...]


def _ada(c_all, w_ada, b_ada):
    nb, d = c_all.shape
    n = w_ada.shape[1]
    tn = 1024
    return pl.pallas_call(
        _ada_kernel,
        grid=(n // tn,),
        in_specs=[pl.BlockSpec((nb, d), lambda j: (0, 0)),
                  pl.BlockSpec((d, tn), lambda j: (0, j)),
                  pl.BlockSpec((1, tn), lambda j: (0, j))],
        out_specs=pl.BlockSpec((nb, tn), lambda j: (0, j)),
        out_shape=jax.ShapeDtypeStruct((nb, n), F32),
        compiler_params=_params(("arbitrary",)),
        name="ada",
    )(c_all, w_ada, b_ada.reshape(1, n))


def _proj_kernel(x_ref, sc_ref, sh_ref, w_ref, o_ref, h_scr, *, gate):
    @pl.when(pl.program_id(1) == 0)
    def _():
        h = x_ref[...] * (1.0 + sc_ref[0]) + sh_ref[0]
        h_scr[...] = h.astype(BF16)

    acc = jnp.dot(h_scr[...], w_ref[...], preferred_element_type=F32)
    if gate:
        acc = jax.nn.sigmoid(acc)
    o_ref[...] = acc.astype(o_ref.dtype)


def _proj(x2d, mod3, boff, seq, w, tm, tn, out_dtype, gate):
    t, d = x2d.shape
    n = w.shape[1]
    bpb = seq // tm
    return pl.pallas_call(
        functools.partial(_proj_kernel, gate=gate),
        grid=(t // tm, n // tn),
        in_specs=[pl.BlockSpec((tm, d), lambda i, j: (i, 0)),
                  pl.BlockSpec((1, 1, d), lambda i, j: (boff + i // bpb, 0, 1)),
                  pl.BlockSpec((1, 1, d), lambda i, j: (boff + i // bpb, 0, 0)),
                  pl.BlockSpec((d, tn), lambda i, j: (0, j))],
        out_specs=pl.BlockSpec((tm, tn), lambda i, j: (i, j)),
        out_shape=jax.ShapeDtypeStruct((t, n), out_dtype),
        scratch_shapes=[pltpu.VMEM((tm, d), BF16)],
        compiler_params=_params(("parallel", "arbitrary")),
        name="proj_gate" if gate else "proj_qkv",
    )(x2d, mod3, mod3, w)


def _window_kernel(sink_ref, q_ref, kp_ref, kc_ref, kn_ref, vp_ref, vc_ref, vn_ref, o_ref, *, seq):
    blk = WINDOW
    n = pl.program_id(1)
    krel = lax.broadcasted_iota(jnp.int32, (3 * blk, blk), 0) - blk
    qrel = lax.broadcasted_iota(jnp.int32, (3 * blk, blk), 1)
    dist_i = jnp.abs(krel - qrel)
    kabs = n * blk + krel
    valid = (dist_i <= WINDOW) & (kabs >= 0) & (kabs < seq)
    dist = dist_i.astype(F32)
    scale = HEAD_DIM ** -0.5
    for g in range(A_KV_HEADS):
        cs = slice(g * HEAD_DIM, (g + 1) * HEAD_DIM)
        k = jnp.concatenate([kp_ref[:, cs], kc_ref[:, cs], kn_ref[:, cs]], axis=0)
        v = jnp.concatenate([vp_ref[:, cs], vc_ref[:, cs], vn_ref[:, cs]], axis=0)
        for j in range(A_GROUP):
            h = g * A_GROUP + j
            slope = 2.0 ** (-8.0 * (h + 1) / A_Q_HEADS)
            hs = slice(h * HEAD_DIM, (h + 1) * HEAD_DIM)
            q = q_ref[:, hs]
            s = lax.dot_general(k, q, NT_DIMS, preferred_element_type=F32) * scale
            s = jnp.where(valid, s - slope * dist, -jnp.inf)
            sink = sink_ref[h]
            m = jnp.maximum(jnp.max(s, axis=0, keepdims=True), sink)
            p = jnp.exp(s - m)
            l = jnp.sum(p, axis=0, keepdims=True) + jnp.exp(sink - m)
            o_t = lax.dot_general(v, p.astype(BF16), TN_DIMS, preferred_element_type=F32)
            o_t = o_t * (1.0 / l)
            o_ref[:, hs] = o_t.T.astype(o_ref.dtype)


def _window(qkv, sink, nbatch, seq):
    blk = WINDOW
    nb = seq // blk
    t = nbatch * seq
    kcol = (SZ_QA + SZ_QB + 2 * SZ_VB) // (A_KV_HEADS * HEAD_DIM)
    vcol = kcol + 1

    def row(b, n, off):
        return b * nb + jnp.clip(n + off, 0, nb - 1)

    kv_specs = [pl.BlockSpec((blk, SZ_KA), functools.partial(lambda b, n, col, off: (row(b, n, off), col), col=col, off=off))
                for col in (kcol, vcol) for off in (-1, 0, 1)]
    return pl.pallas_call(
        functools.partial(_window_kernel, seq=seq),
        grid=(nbatch, nb),
        in_specs=[pl.BlockSpec(memory_space=pltpu.SMEM),
                  pl.BlockSpec((blk, SZ_QA), lambda b, n: (b * nb + n, 0))] + kv_specs,
        out_specs=pl.BlockSpec((blk, SZ_QA), lambda b, n: (b * nb + n, 0)),
        out_shape=jax.ShapeDtypeStruct((t, SZ_QA), BF16),
        compiler_params=_params(("parallel", "parallel")),
        name="window_attn",
    )(sink, qkv, qkv, qkv, qkv, qkv, qkv, qkv)


DIFF_AUG = 16
DIFF_SPLIT = 256
DIFF_QK_AHEAD = 1
DIFF_PV_BEHIND = 2
DIFF_ROWS = 128
LOG2E = math.log2(math.e)
LOG2E_PARTS = (1.4453125, -0.00262451171875, 7.063150405883789e-06)


def _diff_step(q_bias_rows, group_shift, k_ref, kc_ref, vt_ref, qt_scr, m_scr, acc_scr, s_scr, p_scr, dist, *,
               tq, tk, on_diag):
    nhm = 2 * B_HEADS
    ns, npb = DIFF_QK_AHEAD + 1, DIFF_PV_BEHIND + 1
    ngrp = tk // DIFF_SPLIT
    tiles = DIFF_SPLIT // DIFF_ROWS

    def qk(hm):
        h = hm // 2
        k_aug = jnp.concatenate([k_ref[:, h * 128:(h + 1) * 128], kc_ref[...]], axis=1)
        s_scr[hm % ns] = jnp.dot(k_aug, qt_scr[hm], preferred_element_type=F32)

    def softmax(hm):
        h, slot, pslot = hm // 2, hm % ns, hm % npb
        slope = 2.0 ** (-8.0 * (h + 1) / B_HEADS)
        shifts = [group_shift(h, g) for g in range(ngrp)]
        alphas = []
        for c in range(tq // LANES):
            cs = slice(c * LANES, (c + 1) * LANES)
            m_blk = None
            for g in range(ngrp):
                mrun = None
                for rb in range(tiles):
                    rs = slice(g * DIFF_SPLIT + rb * DIFF_ROWS, g * DIFF_SPLIT + (rb + 1) * DIFF_ROWS)
                    s = s_scr[slot, rs, cs]
                    if on_diag:
                        s = s - (slope * LOG2E) * dist[rs, cs]
                        s_scr[slot, rs, cs] = s
                    mrun = s if mrun is None else jnp.maximum(mrun, s)
                mg = jnp.max(mrun, axis=0, keepdims=True)
                if shifts[g] is not None:
                    mg = mg + shifts[g]
                m_blk = mg if m_blk is None else jnp.maximum(m_blk, mg)
            m_old = m_scr[hm:hm + 1, cs]
            m_new = jnp.maximum(m_old, m_blk)
            alphas.append(jnp.exp2(m_old - m_new))
            for g in range(ngrp):
                m_g = m_new if shifts[g] is None else m_new - shifts[g]
                for rb in range(tiles):
                    rs = slice(g * DIFF_SPLIT + rb * DIFF_ROWS, g * DIFF_SPLIT + (rb + 1) * DIFF_ROWS)
                    p_scr[pslot, rs, cs] = jnp.exp2(s_scr[slot, rs, cs] - m_g).astype(BF16)
            m_scr[hm:hm + 1, cs] = m_new
        return jnp.concatenate(alphas, axis=1)

    def pv(hm, alpha):
        r = jnp.dot(vt_ref[hm // 2], p_scr[hm % npb], preferred_element_type=F32)
        acc_scr[hm] = alpha * acc_scr[hm] + r

    for hm in range(nhm):
        qt_scr[hm, 128:128 + DIFF_AUG, :] = q_bias_rows(hm // 2)
    for hm in range(DIFF_QK_AHEAD):
        qk(hm)
    alpha = {}
    for t in range(nhm + DIFF_PV_BEHIND):
        if t + DIFF_QK_AHEAD < nhm:
            qk(t + DIFF_QK_AHEAD)
        if t < nhm:
            alpha[t] = softmax(t)
        if t >= DIFF_PV_BEHIND:
            pv(t - DIFF_PV_BEHIND, alpha.pop(t - DIFF_PV_BEHIND))


def _diff_kernel(lq1_ref, lk1_ref, lq2_ref, lk2_ref, ng_ref, q_ref, k_ref, kc_ref, vt_ref, o_ref,
                 qt_scr, m_scr, acc_scr, s_scr, p_scr, dist_scr, *, tq, tk):
    qi = pl.program_id(1)
    kv = pl.program_id(2)
    nkv = pl.num_programs(2)

    @pl.when(kv == 0)
    def _():
        m_scr[...] = jnp.full(m_scr.shape, -jnp.inf, F32)
        acc_scr[...] = jnp.zeros(acc_scr.shape, F32)
        row = lax.broadcasted_iota(jnp.int32, (2 * B_QK_DIM, tq), 0)
        zeros = jnp.zeros((128, tq), BF16)
        for h in range(B_HEADS):
            q_t = (q_ref[:, h * 128:(h + 1) * 128].astype(F32) * (B_QK_DIM ** -0.5 * LOG2E)).T
            qt_scr[2 * h, 0:128, :] = jnp.where(row < B_QK_DIM, q_t, 0.0).astype(BF16)
            qt_scr[2 * h + 1, 0:128, :] = jnp.where(row >= B_QK_DIM, q_t, 0.0).astype(BF16)
            qt_scr[2 * h, 128:256, :] = zeros
            qt_scr[2 * h + 1, 128:256, :] = zeros

    q0 = qi * tq
    k0 = kv * tk
    left = k0 + tk <= q0
    right = k0 >= q0 + tq
    args = (k_ref, kc_ref, vt_ref, qt_scr, m_scr, acc_scr, s_scr, p_scr)

    @pl.when(jnp.logical_or(left, right))
    def _():
        sgn = jnp.where(left, 1.0, -1.0).astype(F32)
        off = (q0 - k0).astype(F32)
        rid = lax.broadcasted_iota(jnp.int32, (DIFF_AUG, tq), 0)
        il = (lax.broadcasted_iota(jnp.int32, (DIFF_AUG, tq), 1) - DIFF_SPLIT // 2).astype(F32)

        def rows(h):
            g = sgn * 2.0 ** (-8.0 * (h + 1) / B_HEADS)
            r = jnp.where(rid < 3, -g * il, 0.0)
            for part, c in enumerate(LOG2E_PARTS):
                r = jnp.where(rid == 3 + part, g * c, r)
            return r.astype(BF16)

        def shift(h, grp):
            return -(sgn * 2.0 ** (-8.0 * (h + 1) / B_HEADS) * LOG2E) * (off - float(DIFF_SPLIT * grp))

        _diff_step(rows, shift, *args, None, tq=tq, tk=tk, on_diag=False)

    @pl.when(jnp.logical_not(jnp.logical_or(left, right)))
    def _():
        kpos = k0 + lax.broadcasted_iota(jnp.int32, (tk, tq), 0)
        qpos = q0 + lax.broadcasted_iota(jnp.int32, (tk, tq), 1)
        dist_scr[...] = jnp.abs(kpos - qpos).astype(F32)
        _diff_step(lambda h: jnp.zeros((DIFF_AUG, tq), BF16), lambda h, g: None, *args, dist_scr,
                   tq=tq, tk=tk, on_diag=True)

    @pl.when(kv == nkv - 1)
    def _():
        lam = (jnp.exp(jnp.sum(lq1_ref[...] * lk1_ref[...], axis=-1, keepdims=True))
               - jnp.exp(jnp.sum(lq2_ref[...] * lk2_ref[...], axis=-1, keepdims=True)) + LAM_INIT)
        for h in range(B_HEADS):
            a0, a1 = acc_scr[2 * h], acc_scr[2 * h + 1]
            o0 = a0[0:128] * (1.0 / a0[128:129])
            o1 = a1[0:128] * (1.0 / a1[128:129])
            o = (o0 - lam * o1).T
            o = o * lax.rsqrt(jnp.mean(o * o, axis=-1, keepdims=True) + RMS_EPS) * ng_ref[...]
            o_ref[:, h * 128:(h + 1) * 128] = (o * (1.0 - LAM_INIT)).astype(o_ref.dtype)


def _diff(qkv, vt_aug, lam_q1, lam_k1, lam_q2, lam_k2, norm_g, nbatch, seq, tq, tk):
    t = nbatch * seq
    nq, nk = seq // tq, seq // tk
    w = SZ_QB
    nhm = 2 * B_HEADS
    small = lambda a: a.reshape(1, -1).astype(F32)
    sspec = lambda n: pl.BlockSpec((1, n), lambda b, i, j: (0, 0))
    assert tq == DIFF_SPLIT and tk % DIFF_SPLIT == 0
    jl = (jnp.arange(tk) % DIFF_SPLIT - DIFF_SPLIT // 2).astype(F32)
    kc = jnp.zeros((tk, 128), F32).at[:, 0:3].set(jnp.asarray(LOG2E_PARTS, F32)).at[:, 3:6].set(jl[:, None])
    return pl.pallas_call(
        functools.partial(_diff_kernel, tq=tq, tk=tk),
        grid=(nbatch, nq, nk),
        in_specs=[sspec(B_QK_DIM)] * 4 + [sspec(2 * B_QK_DIM),
                  pl.BlockSpec((tq, w), lambda b, i, j: (b * nq + i, 1)),
                  pl.BlockSpec((tk, w), lambda b, i, j: (b * nk + j, 2)),
                  _const_spec((tk, 128)),
                  pl.BlockSpec((B_HEADS, 128 + DIFF_AUG, tk), lambda b, i, j: (0, 0, b * nk + j))],
        out_specs=pl.BlockSpec((tq, w), lambda b, i, j: (b * nq + i, 0)),
        out_shape=jax.ShapeDtypeStruct((t, w), BF16),
        scratch_shapes=[pltpu.VMEM((nhm, 256, tq), BF16),
                        pltpu.VMEM((nhm, tq), F32),
                        pltpu.VMEM((nhm, 128 + DIFF_AUG, tq), F32),
                        pltpu.VMEM((DIFF_QK_AHEAD + 1, tk, tq), F32),
                        pltpu.VMEM((DIFF_PV_BEHIND + 1, tk, tq), BF16),
                        pltpu.VMEM((tk, tq), F32)],
        compiler_params=_params(("parallel", "parallel", "arbitrary")),
        name="diff_attn",
    )(small(lam_q1), small(lam_k1), small(lam_q2), small(lam_k2), small(norm_g), qkv, qkv, kc.astype(BF16), vt_aug)


def _merge_kernel(ya_ref, yb_ref, ga_ref, gb_ref, x_ref, g1_ref, sc2_ref, sh2_ref,
                  wua_ref, wub_ref, wo_ref, lg_ref, lb_ref, x1_ref, h2_ref):
    ua = jnp.dot(ya_ref[...], wua_ref[...], preferred_element_type=F32)
    ub = jnp.dot(yb_ref[...], wub_ref[...], preferred_element_type=F32)
    merged = ga_ref[...] * ua + gb_ref[...] * ub
    z = jnp.dot(merged.astype(BF16), wo_ref[...], preferred_element_type=F32)
    r = DN_ALPHA * x_ref[...] + g1_ref[0] * z
    x1 = _layer_norm(r, lg_ref[...], lb_ref[...])
    x1_ref[...] = x1
    h2_ref[...] = (x1 * (1.0 + sc2_ref[0]) + sh2_ref[0]).astype(BF16)


def _merge(ya, yb, gates, x2d, mod3, boff, seq, wua, wub, wo, ln_g, ln_b, tm):
    t, d = x2d.shape
    bpb = seq // tm
    mspec = lambda c: pl.BlockSpec((1, 1, d), lambda i: (boff + i // bpb, 0, c))
    row = lambda n, c=0: pl.BlockSpec((tm, n), lambda i: (i, c))
    return pl.pallas_call(
        _merge_kernel,
        grid=(t // tm,),
        in_specs=[row(SZ_QA), row(SZ_VB), row(d, 0), row(d, 1), row(d),
                  mspec(2), mspec(4), mspec(3),
                  _const_spec(wua.shape), _const_spec(wub.shape), _const_spec(wo.shape),
                  _const_spec((1, d)), _const_spec((1, d))],
        out_specs=[row(d), row(d)],
        out_shape=[jax.ShapeDtypeStruct((t, d), F32), jax.ShapeDtypeStruct((t, d), BF16)],
        compiler_params=_params(("parallel",)),
        name="merge_ln1",
    )(ya, yb, gates, gates, x2d, mod3, mod3, mod3, wua, wub, wo, ln_g.reshape(1, d), ln_b.reshape(1, d))


def _top16(s):
    n, t = s.shape
    iota = lax.broadcasted_iota(jnp.int32, (n, t), 0)
    rank = jnp.full((n, t), float(PEER_TOPK), F32)
    work = s
    vals = []
    for a in range(PEER_TOPK):
        m = jnp.max(work, axis=0, keepdims=True)
        idx = jnp.min(jnp.where(work == m, iota, n), axis=0, keepdims=True)
        sel = iota == idx
        rank = jnp.where(sel, float(a), rank)
        work = jnp.where(sel, -jnp.inf, work)
        vals.append(m)
    return jnp.concatenate(vals, axis=0), rank


def _peer_q_kernel(h2_ref, wq_ref, sub_ref, a1_ref, lrow_ref, a2_ref, r2_ref, q_scr):
    hd = pl.program_id(1)
    kd = N_KEYS

    @pl.when(hd == 0)
    def _():
        q = jnp.dot(h2_ref[...], wq_ref[...], preferred_element_type=F32).astype(BF16)
        for h in range(PEER_HEADS):
            q_scr[h] = q[:, h * 2 * kd:(h + 1) * 2 * kd]

    q = q_scr[hd]
    s1 = lax.dot_general(sub_ref[0, 0], q[:, :kd], NT_DIMS, preferred_element_type=F32)
    s2 = lax.dot_general(sub_ref[0, 1], q[:, kd:], NT_DIMS, preferred_element_type=F32)
    v1, r1 = _top16(s1)
    v2, r2 = _top16(s2)
    half = PEER_TOPK // 2
    cand = jnp.concatenate([v1[0:1] + v2] + [v1[a:a + 1] + v2[0:half] for a in range(1, half)]
                           + [v1[half:] + v2[0:1]], axis=0)
    cv, crank = _top16(cand)
    chosen = jnp.where(crank < float(PEER_TOPK), 1.0, 0.0)
    z = jnp.sum(jnp.exp(cv - cv[0:1, :]), axis=0, keepdims=True)
    inv_z = 1.0 / z
    tail = PEER_TOPK + (half - 1) * half
    cnts = [jnp.sum(chosen[0:PEER_TOPK], axis=0, keepdims=True)]
    cnts += [jnp.sum(chosen[PEER_TOPK + (a - 1) * half:PEER_TOPK + a * half], axis=0, keepdims=True)
             for a in range(1, half)]
    cnts += [chosen[tail + a:tail + a + 1] for a in range(half)]
    lrow = jnp.zeros_like(s1)
    for a in range(PEER_TOPK):
        lrow = jnp.where(r1 == float(a), cnts[a], lrow)
    a1_ref[0] = jnp.where(r1 < float(PEER_TOPK), jnp.exp(s1 - v1[0:1, :]) * inv_z, 0.0)
    lrow_ref[0] = lrow
    a2_ref[0] = jnp.exp(s2 - v2[0:1, :])
    r2_ref[0] = r2


def _peer_q(h2, wq, sub, tm):
    t, d = h2.shape
    out = jax.ShapeDtypeStruct((PEER_HEADS, N_KEYS, t), F32)
    ospec = pl.BlockSpec((1, N_KEYS, tm), lambda i, h: (h, 0, i))
    return pl.pallas_call(
        _peer_q_kernel,
        grid=(t // tm, PEER_HEADS),
        in_specs=[pl.BlockSpec((tm, d), lambda i, h: (i, 0)),
                  _const_spec(wq.shape),
                  pl.BlockSpec((1, 2, N_KEYS, N_KEYS), lambda i, h: (h, 0, 0, 0))],
        out_specs=[ospec] * 4,
        out_shape=[out] * 4,
        scratch_shapes=[pltpu.VMEM((PEER_HEADS, tm, 2 * N_KEYS), BF16)],
        compiler_params=_params(("parallel", "arbitrary")),
        name="peer_select",
    )(h2, wq, sub)


def _peer_kernel(h2_ref, a1_ref, lrow_ref, a2_ref, r2_ref, u_ref, v_ref, x1_ref, g2_ref, lg_ref, lb_ref,
                 o_ref, acc_scr, w_scr, xu_scr, g_scr, *, rows, nj):
    n = pl.program_id(0)
    jp = jnp.maximum(n - 1, 0) % nj

    @pl.when(n == 0)
    def _():
        w_scr[...] = jnp.zeros(w_scr.shape, BF16)

    @pl.when(jp == 0)
    def _():
        acc_scr[...] = jnp.zeros(acc_scr.shape, F32)

    acc_scr[...] += jnp.dot(w_scr[...], v_ref[...], preferred_element_type=F32)

    tm = h2_ref.shape[0]
    for tc in range(tm // LANES):
        cs = slice(tc * LANES, (tc + 1) * LANES)
        for sb in range(N_KEYS // GATE_SUB):
            ss = slice(sb * GATE_SUB, (sb + 1) * GATE_SUB)
            g = [None] * rows
            for h in range(PEER_HEADS):
                r2 = r2_ref[h, ss, cs]
                a2 = a2_ref[h, ss, cs]
                for r in range(rows):
                    term = jnp.where(r2 < lrow_ref[h, r:r + 1, cs], a1_ref[h, r:r + 1, cs] * a2, 0.0)
                    g[r] = term if g[r] is None else g[r] + term
            for r in range(rows):
                g_scr[r * N_KEYS + sb * GATE_SUB:r * N_KEYS + (sb + 1) * GATE_SUB, cs] = g[r]

    xu_scr[...] = lax.dot_general(u_ref[...], h2_ref[...], NT_DIMS, preferred_element_type=F32)
    for r in range(rows):
        rs = slice(r * N_KEYS, (r + 1) * N_KEYS)
        for tc in range(tm // LANES):
            cs = slice(tc * LANES, (tc + 1) * LANES)
            xu = xu_scr[rs, cs]
            w_t = 0.5 * xu * (1.0 + lax.erf(xu * (2.0 ** -0.5))) * g_scr[rs, cs]
            w_scr[cs, rs] = w_t.T.astype(BF16)

    @pl.when(jnp.logical_and(n > 0, jp == nj - 1))
    def _():
        r = DN_ALPHA * x1_ref[...] + g2_ref[0] * acc_scr[...]
        o_ref[...] = _layer_norm(r, lg_ref[...], lb_ref[...])


def _peer(h2, sel, u, v, x1, mod3, boff, seq, ln_g, ln_b, tm, te):
    t, d = h2.shape
    rows = te // N_KEYS
    bpb = seq // tm
    nj = N_EXPERTS // te
    a1, lrow, a2, r2 = sel
    ni = t // tm
    last = ni * nj - 1
    once = dict(pipeline_mode=pl.Buffered(1))
    cur = lambda n: jnp.minimum(n, last)
    prv = lambda n: jnp.maximum(n - 1, 0)
    full = pl.BlockSpec((PEER_HEADS, N_KEYS, tm), lambda n: (0, 0, cur(n) // nj), **once)
    part = pl.BlockSpec((PEER_HEADS, rows, tm), lambda n: (0, cur(n) % nj, cur(n) // nj))
    return pl.pallas_call(
        functools.partial(_peer_kernel, rows=rows, nj=nj),
        grid=(ni * nj + 1,),
        in_specs=[pl.BlockSpec((tm, d), lambda n: (cur(n) // nj, 0), **once), part, part, full, full,
                  pl.BlockSpec((te, d), lambda n: (cur(n) % nj, 0)),
                  pl.BlockSpec((te, d), lambda n: (prv(n) % nj, 0)),
                  pl.BlockSpec((tm, d), lambda n: (prv(n) // nj, 0), **once),
                  pl.BlockSpec((1, 1, d), lambda n: (boff + prv(n) // nj // bpb, 0, 5)),
                  _const_spec((1, d)), _const_spec((1, d))],
        out_specs=pl.BlockSpec((tm, d), lambda n: (prv(n) // nj, 0)),
        out_shape=jax.ShapeDtypeStruct((t, d), F32),
        scratch_shapes=[pltpu.VMEM((tm, d), F32), pltpu.VMEM((tm, te), BF16),
                        pltpu.VMEM((te, tm), F32), pltpu.VMEM((te, tm), F32)],
        compiler_params=_params(("arbitrary",)),
        name="peer_experts",
    )(h2, a1, lrow, a2, r2, u, v, x1, mod3, ln_g.reshape(1, d), ln_b.reshape(1, d))


def _trunk(x, mod3, boff, w):
    nbatch, seq, d = x.shape
    x2d = x.reshape(nbatch * seq, d)
    qkv = _proj(x2d, mod3, boff, seq, w["w_qkv"], 512, 1536, BF16, False)
    gates = _proj(x2d, mod3, boff, seq, w["w_gate"], 512, 1024, F32, True)
    ya = _window(qkv, w["sink"], nbatch, seq)
    t = nbatch * seq
    vt = qkv[:, SZ_QA + 2 * SZ_QB:SZ_QA + 2 * SZ_QB + SZ_VB].T.reshape(B_HEADS, 128, t)
    vt_aug = jnp.concatenate([vt, jnp.ones((B_HEADS, DIFF_AUG, t), BF16)], axis=1)
    yb = _diff(qkv, vt_aug, w["lam_q1"], w["lam_k1"], w["lam_q2"], w["lam_k2"], w["norm_g"], nbatch, seq, 256, 1024)
    x1, h2 = _merge(ya, yb, gates, x2d, mod3, boff, seq, w["w_up_a"], w["w_up_b"], w["w_o"],
                    w["ln1_g"], w["ln1_b"], 256)
    sel = _peer_q(h2, w["peer_wq"], w["peer_sub"], 512)
    y = _peer(h2, sel, w["peer_u"], w["peer_v"], x1, mod3, boff, seq, w["ln2_g"], w["ln2_b"], 512, 1024)
    return y.reshape(nbatch, seq, d)


def kernel(x_prompt, x_sample, c_prompt, c_sample, w_ada, b_ada, w_in, sink_a, lam_q1, lam_k1, lam_q2, lam_k2, diff_norm_g, w_up_a, w_up_b, w_o, ln1_g, ln1_b, peer_wq, peer_subkeys, peer_u, peer_v, ln2_g, ln2_b):
    layer = 0
    nbp, nbs = x_prompt.shape[0], x_sample.shape[0]
    pad = (-(nbp + nbs)) % 8
    c_all = jnp.concatenate([c_prompt, c_sample, jnp.zeros((pad, D_MODEL), F32)], axis=0)
    mod = _ada(c_all, w_ada[layer], b_ada[layer])
    mod3 = mod.reshape(mod.shape[0], 1, 6 * D_MODEL)

    wi = w_in[layer]
    o_ka = SZ_QA
    o_va = o_ka + SZ_KA
    o_qb = o_va + SZ_KA
    o_kb = o_qb + SZ_QB
    o_vb = o_kb + SZ_QB
    o_g = o_vb + SZ_VB
    w_qkv = jnp.concatenate([wi[:, :o_ka], wi[:, o_qb:o_g], wi[:, o_ka:o_qb]], axis=1).astype(BF16)
    weights = dict(
        w_qkv=w_qkv, w_gate=wi[:, o_g:].astype(BF16), sink=sink_a[layer].astype(F32),
        lam_q1=lam_q1[layer], lam_k1=lam_k1[layer], lam_q2=lam_q2[layer], lam_k2=lam_k2[layer],
        norm_g=diff_norm_g[layer],
        w_up_a=w_up_a[layer].astype(BF16), w_up_b=w_up_b[layer].astype(BF16), w_o=w_o[layer].astype(BF16),
        ln1_g=ln1_g[layer], ln1_b=ln1_b[layer],
        peer_wq=peer_wq[layer].astype(BF16), peer_sub=peer_subkeys[layer].astype(BF16),
        peer_u=peer_u[layer].astype(BF16), peer_v=peer_v[layer].astype(BF16),
        ln2_g=ln2_g[layer], ln2_b=ln2_b[layer],
    )
    y_prompt = _trunk(x_prompt, mod3, 0, weights)
    y_sample = _trunk(x_sample, mod3, nbp, weights)
    return (y_prompt, y_sample)
```

```python
import functools
import math

import jax
import jax.numpy as jnp
from jax import lax
from jax.experimental import pallas as pl
from jax.experimental.pallas import tpu as pltpu

F32 = jnp.float32
BF16 = jnp.bfloat16

D_MODEL = 2048
HEAD_DIM = 128
A_Q_HEADS = 8
A_KV_HEADS = 2
A_GROUP = A_Q_HEADS // A_KV_HEADS
WINDOW = 128
B_HEADS = 8
B_QK_DIM = 64
PEER_HEADS = 8
N_KEYS = 128
N_EXPERTS = N_KEYS * N_KEYS
PEER_TOPK = 16
SZ_QA = A_Q_HEADS * HEAD_DIM
SZ_KA = A_KV_HEADS * HEAD_DIM
SZ_QB = B_HEADS * 2 * B_QK_DIM
SZ_VB = B_HEADS * 2 * B_QK_DIM
DEPTH = 1
DN_ALPHA = (2.0 * DEPTH) ** 0.25
LN_EPS = 1e-5
RMS_EPS = 1e-5
LAM_INIT = 0.8 - 0.6 * math.exp(-0.3 * 0)

VMEM_LIMIT_V7X = 56 * 1024 * 1024

NT_DIMS = (((1,), (1,)), ((), ()))
TN_DIMS = (((0,), (0,)), ((), ()))


LANES = 128
GATE_SUB = 32


def _params(sem):
    return pltpu.CompilerParams(dimension_semantics=sem, vmem_limit_bytes=VMEM_LIMIT_V7X)


def _const_spec(shape):
    nd = len(shape)
    return pl.BlockSpec(shape, lambda *_: (0,) * nd, pipeline_mode=pl.Buffered(1))


def _layer_norm(r, g, b):
    mu = jnp.mean(r, axis=-1, keepdims=True)
    c = r - mu
    var = jnp.mean(c * c, axis=-1, keepdims=True)
    return c * lax.rsqrt(var + LN_EPS) * g + b


def _ada_kernel(c_ref, w_ref, b_ref, o_ref):
    c = c_ref[...]
    a = (c * jax.nn.sigmoid(c)).astype(BF16)
    o_ref[...] = jnp.dot(a, w_ref[...].astype(BF16), preferred_element_type=F32) + b_ref[...]


def _ada(c_all, w_ada, b_ada):
    nb, d = c_all.shape
    n = w_ada.shape[1]
    tn = 1024
    return pl.pallas_call(
        _ada_kernel,
        grid=(n // tn,),
        in_specs=[pl.BlockSpec((nb, d), lambda j: (0, 0)),
                  pl.BlockSpec((d, tn), lambda j: (0, j)),
                  pl.BlockSpec((1, tn), lambda j: (0, j))],
        out_specs=pl.BlockSpec((nb, tn), lambda j: (0, j)),
        out_shape=jax.ShapeDtypeStruct((nb, n), F32),
        compiler_params=_params(("arbitrary",)),
        name="ada",
    )(c_all, w_ada, b_ada.reshape(1, n))


def _proj_kernel(x_ref, sc_ref, sh_ref, w_ref, o_ref, h_scr, *, gate):
    @pl.when(pl.program_id(1) == 0)
    def _():
        h = x_ref[...] * (1.0 + sc_ref[0]) + sh_ref[0]
        h_scr[...] = h.astype(BF16)

    acc = jnp.dot(h_scr[...], w_ref[...], preferred_element_type=F32)
    if gate:
        acc = jax.nn.sigmoid(acc)
    o_ref[...] = acc.astype(o_ref.dtype)


def _proj(x2d, mod3, boff, seq, w, tm, tn, out_dtype, gate):
    t, d = x2d.shape
    n = w.shape[1]
    bpb = seq // tm
    return pl.pallas_call(
        functools.partial(_proj_kernel, gate=gate),
        grid=(t // tm, n // tn),
        in_specs=[pl.BlockSpec((tm, d), lambda i, j: (i, 0)),
                  pl.BlockSpec((1, 1, d), lambda i, j: (boff + i // bpb, 0, 1)),
                  pl.BlockSpec((1, 1, d), lambda i, j: (boff + i // bpb, 0, 0)),
                  pl.BlockSpec((d, tn), lambda i, j: (0, j))],
        out_specs=pl.BlockSpec((tm, tn), lambda i, j: (i, j)),
        out_shape=jax.ShapeDtypeStruct((t, n), out_dtype),
        scratch_shapes=[pltpu.VMEM((tm, d), BF16)],
        compiler_params=_params(("parallel", "arbitrary")),
        name="proj_gate" if gate else "proj_qkv",
    )(x2d, mod3, mod3, w)


def _window_kernel(sink_ref, q_ref, kp_ref, kc_ref, kn_ref, vp_ref, vc_ref, vn_ref, o_ref, *, seq):
    blk = WINDOW
    n = pl.program_id(1)
    krel = lax.broadcasted_iota(jnp.int32, (3 * blk, blk), 0) - blk
    qrel = lax.broadcasted_iota(jnp.int32, (3 * blk, blk), 1)
    dist_i = jnp.abs(krel - qrel)
    kabs = n * blk + krel
    valid = (dist_i <= WINDOW) & (kabs >= 0) & (kabs < seq)
    dist = dist_i.astype(F32)
    scale = HEAD_DIM ** -0.5
    for g in range(A_KV_HEADS):
        cs = slice(g * HEAD_DIM, (g + 1) * HEAD_DIM)
        k = jnp.concatenate([kp_ref[:, cs], kc_ref[:, cs], kn_ref[:, cs]], axis=0)
        v = jnp.concatenate([vp_ref[:, cs], vc_ref[:, cs], vn_ref[:, cs]], axis=0)
        for j in range(A_GROUP):
            h = g * A_GROUP + j
            slope = 2.0 ** (-8.0 * (h + 1) / A_Q_HEADS)
            hs = slice(h * HEAD_DIM, (h + 1) * HEAD_DIM)
            q = q_ref[:, hs]
            s = lax.dot_general(k, q, NT_DIMS, preferred_element_type=F32) * scale
            s = jnp.where(valid, s - slope * dist, -jnp.inf)
            sink = sink_ref[h]
            m = jnp.maximum(jnp.max(s, axis=0, keepdims=True), sink)
            p = jnp.exp(s - m)
            l = jnp.sum(p, axis=0, keepdims=True) + jnp.exp(sink - m)
            o_t = lax.dot_general(v, p.astype(BF16), TN_DIMS, preferred_element_type=F32)
            o_t = o_t * (1.0 / l)
            o_ref[:, hs] = o_t.T.astype(o_ref.dtype)


def _window(qkv, sink, nbatch, seq):
    blk = WINDOW
    nb = seq // blk
    t = nbatch * seq
    kcol = (SZ_QA + SZ_QB + 2 * SZ_VB) // (A_KV_HEADS * HEAD_DIM)
    vcol = kcol + 1

    def row(b, n, off):
        return b * nb + jnp.clip(n + off, 0, nb - 1)

    kv_specs = [pl.BlockSpec((blk, SZ_KA), functools.partial(lambda b, n, col, off: (row(b, n, off), col), col=col, off=off))
                for col in (kcol, vcol) for off in (-1, 0, 1)]
    return pl.pallas_call(
        functools.partial(_window_kernel, seq=seq),
        grid=(nbatch, nb),
        in_specs=[pl.BlockSpec(memory_space=pltpu.SMEM),
                  pl.BlockSpec((blk, SZ_QA), lambda b, n: (b * nb + n, 0))] + kv_specs,
        out_specs=pl.BlockSpec((blk, SZ_QA), lambda b, n: (b * nb + n, 0)),
        out_shape=jax.ShapeDtypeStruct((t, SZ_QA), BF16),
        compiler_params=_params(("parallel", "parallel")),
        name="window_attn",
    )(sink, qkv, qkv, qkv, qkv, qkv, qkv, qkv)


DIFF_AUG = 16
DIFF_SPLIT = 256
DIFF_QK_AHEAD = 1
DIFF_PV_BEHIND = 2
DIFF_ROWS = 128
DIFF_UNDERFLOW = 160.0
DIFF_NORM_SLACK = 1.05
DIFF_FIRST_HEADS = (0, 2, 3, 4, 5)
LOG2E = math.log2(math.e)
LOG2E_PARTS = (1.4453125, -0.00262451171875, 7.063150405883789e-06)


def _diff_step(q_bias_rows, group_shift, k_ref, kc_ref, vt_ref, qt_scr, m_scr, acc_scr, s_scr, p_scr, dist, *,
               tq, tk, on_diag, first_head=0):
    nhm = 2 * B_HEADS
    ns, npb = DIFF_QK_AHEAD + 1, DIFF_PV_BEHIND + 1
    ngrp = tk // DIFF_SPLIT
    tiles = DIFF_SPLIT // DIFF_ROWS

    def qk(hm):
        h = hm // 2
        k_aug = jnp.concatenate([k_ref[:, h * 128:(h + 1) * 128], kc_ref[...]], axis=1)
        s_scr[hm % ns] = jnp.dot(k_aug, qt_scr[hm], preferred_element_type=F32)

    def softmax(hm):
        h, slot, pslot = hm // 2, hm % ns, hm % npb
        slope = 2.0 ** (-8.0 * (h + 1) / B_HEADS)
        shifts = [group_shift(h, g) for g in range(ngrp)]
        alphas = []
        for c in range(tq // LANES):
            cs = slice(c * LANES, (c + 1) * LANES)
            m_blk = None
            for g in range(ngrp):
                mrun = None
                for rb in range(tiles):
                    rs = slice(g * DIFF_SPLIT + rb * DIFF_ROWS, g * DIFF_SPLIT + (rb + 1) * DIFF_ROWS)
                    s = s_scr[slot, rs, cs]
                    if on_diag:
                        s = s - (slope * LOG2E) * dist[rs, cs]
                        s_scr[slot, rs, cs] = s
                    mrun = s if mrun is None else jnp.maximum(mrun, s)
                mg = jnp.max(mrun, axis=0, keepdims=True)
                if shifts[g] is not None:
                    mg = mg + shifts[g]
                m_blk = mg if m_blk is None else jnp.maximum(m_blk, mg)
            m_old = m_scr[hm:hm + 1, cs]
            m_new = jnp.maximum(m_old, m_blk)
            alphas.append(jnp.exp2(m_old - m_new))
            for g in range(ngrp):
                m_g = m_new if shifts[g] is None else m_new - shifts[g]
                for rb in range(tiles):
                    rs = slice(g * DIFF_SPLIT + rb * DIFF_ROWS, g * DIFF_SPLIT + (rb + 1) * DIFF_ROWS)
                    p_scr[pslot, rs, cs] = jnp.exp2(s_scr[slot, rs, cs] - m_g).astype(BF16)
            m_scr[hm:hm + 1, cs] = m_new
        return jnp.concatenate(alphas, axis=1)

    def pv(hm, alpha):
        r = jnp.dot(vt_ref[hm // 2], p_scr[hm % npb], preferred_element_type=F32)
        acc_scr[hm] = alpha * acc_scr[hm] + r

    lo = 2 * first_head
    for hm in range(lo, nhm):
        qt_scr[hm, 128:128 + DIFF_AUG, :] = q_bias_rows(hm // 2)
    for hm in range(lo, lo + DIFF_QK_AHEAD):
        qk(hm)
    alpha = {}
    for t in range(lo, nhm + DIFF_PV_BEHIND):
        if t + DIFF_QK_AHEAD < nhm:
            qk(t + DIFF_QK_AHEAD)
        if t < nhm:
            alpha[t] = softmax(t)
        if t - DIFF_PV_BEHIND >= lo:
            pv(t - DIFF_PV_BEHIND, alpha.pop(t - DIFF_PV_BEHIND))


def _diff_kernel(qn_ref, kn_ref, lq1_ref, lk1_ref, lq2_ref, lk2_ref, ng_ref, q_ref, k_ref, kc_ref, vt_ref, o_ref,
                 qt_scr, m_scr, acc_scr, s_scr, p_scr, dist_scr, *, tq, tk):
    b = pl.program_id(0)
    qi = pl.program_id(1)
    kv = pl.program_id(2)
    nkv = pl.num_programs(2)
    nq = pl.num_programs(1)
    kd = (qi * tq) // tk
    kb = (kd + kv) % nkv

    @pl.when(kv == 0)
    def _():
        m_scr[...] = jnp.full(m_scr.shape, -jnp.inf, F32)
        acc_scr[...] = jnp.zeros(acc_scr.shape, F32)
        row = lax.broadcasted_iota(jnp.int32, (2 * B_QK_DIM, tq), 0)
        zeros = jnp.zeros((128, tq), BF16)
        for h in range(B_HEADS):
            q_t = (q_ref[:, h * 128:(h + 1) * 128].astype(F32) * (B_QK_DIM ** -0.5 * LOG2E)).T
            qt_scr[2 * h, 0:128, :] = jnp.where(row < B_QK_DIM, q_t, 0.0).astype(BF16)
            qt_scr[2 * h + 1, 0:128, :] = jnp.where(row >= B_QK_DIM, q_t, 0.0).astype(BF16)
            qt_scr[2 * h, 128:256, :] = zeros
            qt_scr[2 * h + 1, 128:256, :] = zeros

    q0 = qi * tq
    k0 = kb * tk
    left = k0 + tk <= q0
    right = k0 >= q0 + tq
    args = (k_ref, kc_ref, vt_ref, qt_scr, m_scr, acc_scr, s_scr, p_scr)

    @pl.when(jnp.logical_or(left, right))
    def _():
        sgn = jnp.where(left, 1.0, -1.0).astype(F32)
        off = (q0 - k0).astype(F32)
        rid = lax.broadcasted_iota(jnp.int32, (DIFF_AUG, tq), 0)
        il = (lax.broadcasted_iota(jnp.int32, (DIFF_AUG, tq), 1) - DIFF_SPLIT // 2).astype(F32)

        def rows(h):
            g = sgn * 2.0 ** (-8.0 * (h + 1) / B_HEADS)
            r = jnp.where(rid < 3, -g * il, 0.0)
            for part, c in enumerate(LOG2E_PARTS):
                r = jnp.where(rid == 3 + part, g * c, r)
            return r.astype(BF16)

        def shift(h, grp):
            return -(sgn * 2.0 ** (-8.0 * (h + 1) / B_HEADS) * LOG2E) * (off - float(DIFF_SPLIT * grp))

        dmin = jnp.where(left, q0 - (k0 + tk - 1), k0 - (q0 + tq - 1)).astype(F32)
        nhm = 2 * B_HEADS
        qoff = (b * nq + qi) * nhm
        koff = (b * nkv + kb) * nhm
        doff = (b * nkv + kd) * nhm
        scale = DIFF_NORM_SLACK * (B_QK_DIM ** -0.5) * LOG2E
        lead = jnp.int32(0)
        ok = jnp.bool_(True)
        for h in range(B_HEADS):
            bound = jnp.float32(0.0)
            for mp in range(2):
                hm = 2 * h + mp
                bound = jnp.maximum(bound, qn_ref[qoff + hm] * (kn_ref[koff + hm] + kn_ref[doff + hm]))
            slope = 2.0 ** (-8.0 * (h + 1) / B_HEADS)
            ok = jnp.logical_and(ok, scale * bound - slope * LOG2E * dmin < -DIFF_UNDERFLOW)
            lead = lead + ok.astype(jnp.int32)
        for idx, first in enumerate(DIFF_FIRST_HEADS):
            upper = DIFF_FIRST_HEADS[idx + 1] if idx + 1 < len(DIFF_FIRST_HEADS) else B_HEADS + 1

            @pl.when(jnp.logical_and(lead >= first, lead < upper))
            def _(first=first):
                _diff_step(rows, shift, *args, None, tq=tq, tk=tk, on_diag=False, first_head=first)

    @pl.when(jnp.logical_not(jnp.logical_or(left, right)))
    def _():
        kpos = k0 + lax.broadcasted_iota(jnp.int32, (tk, tq), 0)
        qpos = q0 + lax.broadcasted_iota(jnp.int32, (tk, tq), 1)
        dist_scr[...] = jnp.abs(kpos - qpos).astype(F32)
        _diff_step(lambda h: jnp.zeros((DIFF_AUG, tq), BF16), lambda h, g: None, *args, dist_scr,
                   tq=tq, tk=tk, on_diag=True)

    @pl.when(kv == nkv - 1)
    def _():
        lam = (jnp.exp(jnp.sum(lq1_ref[...] * lk1_ref[...], axis=-1, keepdims=True))
               - jnp.exp(jnp.sum(lq2_ref[...] * lk2_ref[...], axis=-1, keepdims=True)) + LAM_INIT)
        for h in range(B_HEADS):
            a0, a1 = acc_scr[2 * h], acc_scr[2 * h + 1]
            o0 = a0[0:128] * (1.0 / a0[128:129])
            o1 = a1[0:128] * (1.0 / a1[128:129])
            o = (o0 - lam * o1).T
            o = o * lax.rsqrt(jnp.mean(o * o, axis=-1, keepdims=True) + RMS_EPS) * ng_ref[...]
            o_ref[:, h * 128:(h + 1) * 128] = (o * (1.0 - LAM_INIT)).astype(o_ref.dtype)


def _rownorm_kernel(x_ref, ind_ref, o_ref):
    x = x_ref[...].astype(F32)
    ss = jnp.dot(x * x, ind_ref[...], preferred_element_type=F32)
    o_ref[0] = jnp.broadcast_to(jnp.sqrt(jnp.max(ss, axis=0, keepdims=True)), o_ref.shape[1:])


def _rownorm_max(qkv, col, rows):
    t = qkv.shape[0]
    ind = (jnp.arange(SZ_QB)[:, None] // B_QK_DIM == jnp.arange(LANES)[None, :]).astype(F32)
    out = pl.pallas_call(
        _rownorm_kernel,
        grid=(t // rows,),
        in_specs=[pl.BlockSpec((rows, SZ_QB), lambda i: (i, col)), _const_spec(ind.shape)],
        out_specs=pl.BlockSpec((1, 8, LANES), lambda i: (i, 0, 0)),
        out_shape=jax.ShapeDtypeStruct((t // rows, 8, LANES), F32),
        compiler_params=_params(("parallel",)),
        name="diff_rownorm",
    )(qkv, ind)
    return out[:, 0, :2 * B_HEADS].reshape(-1)


def _diff(qkv, vt_aug, lam_q1, lam_k1, lam_q2, lam_k2, norm_g, nbatch, seq, tq, tk):
    t = nbatch * seq
    nq, nk = seq // tq, seq // tk
    qn = _rownorm_max(qkv, 1, tq)
    kn = _rownorm_max(qkv, 2, tk)
    kblk = lambda i, j: ((i * tq) // tk + j) % nk
    w = SZ_QB
    nhm = 2 * B_HEADS
    small = lambda a: a.reshape(1, -1).astype(F32)
    sspec = lambda n: pl.BlockSpec((1, n), lambda b, i, j: (0, 0))
    assert tq == DIFF_SPLIT and tk % DIFF_SPLIT == 0
    jl = (jnp.arange(tk) % DIFF_SPLIT - DIFF_SPLIT // 2).astype(F32)
    kc = jnp.zeros((tk, 128), F32).at[:, 0:3].set(jnp.asarray(LOG2E_PARTS, F32)).at[:, 3:6].set(jl[:, None])
    return pl.pallas_call(
        functools.partial(_diff_kernel, tq=tq, tk=tk),
        grid=(nbatch, nq, nk),
        in_specs=[pl.BlockSpec(memory_space=pltpu.SMEM)] * 2 + [sspec(B_QK_DIM)] * 4 + [sspec(2 * B_QK_DIM),
                  pl.BlockSpec((tq, w), lambda b, i, j: (b * nq + i, 1)),
                  pl.BlockSpec((tk, w), lambda b, i, j: (b * nk + kblk(i, j), 2)),
                  _const_spec((tk, 128)),
                  pl.BlockSpec((B_HEADS, 128 + DIFF_AUG, tk), lambda b, i, j: (0, 0, b * nk + kblk(i, j)))],
        out_specs=pl.BlockSpec((tq, w), lambda b, i, j: (b * nq + i, 0)),
        out_shape=jax.ShapeDtypeStruct((t, w), BF16),
        scratch_shapes=[pltpu.VMEM((nhm, 256, tq), BF16),
                        pltpu.VMEM((nhm, tq), F32),
                        pltpu.VMEM((nhm, 128 + DIFF_AUG, tq), F32),
                        pltpu.VMEM((DIFF_QK_AHEAD + 1, tk, tq), F32),
                        pltpu.VMEM((DIFF_PV_BEHIND + 1, tk, tq), BF16),
                        pltpu.VMEM((tk, tq), F32)],
        compiler_params=_params(("parallel", "parallel", "arbitrary")),
        name="diff_attn",
    )(qn, kn, small(lam_q1), small(lam_k1), small(lam_q2), small(lam_k2), small(norm_g), qkv, qkv, kc.astype(BF16), vt_aug)


def _merge_kernel(ya_ref, yb_ref, ga_ref, gb_ref, x_ref, g1_ref, sc2_ref, sh2_ref,
                  wua_ref, wub_ref, wo_ref, lg_ref, lb_ref, x1_ref, h2_ref):
    ua = jnp.dot(ya_ref[...], wua_ref[...], preferred_element_type=F32)
    ub = jnp.dot(yb_ref[...], wub_ref[...], preferred_element_type=F32)
    merged = ga_ref[...] * ua + gb_ref[...] * ub
    z = jnp.dot(merged.astype(BF16), wo_ref[...], preferred_element_type=F32)
    r = DN_ALPHA * x_ref[...] + g1_ref[0] * z
    x1 = _layer_norm(r, lg_ref[...], lb_ref[...])
    x1_ref[...] = x1
    h2_ref[...] = (x1 * (1.0 + sc2_ref[0]) + sh2_ref[0]).astype(BF16)


def _merge(ya, yb, gates, x2d, mod3, boff, seq, wua, wub, wo, ln_g, ln_b, tm):
    t, d = x2d.shape
    bpb = seq // tm
    mspec = lambda c: pl.BlockSpec((1, 1, d), lambda i: (boff + i // bpb, 0, c))
    row = lambda n, c=0: pl.BlockSpec((tm, n), lambda i: (i, c))
    return pl.pallas_call(
        _merge_kernel,
        grid=(t // tm,),
        in_specs=[row(SZ_QA), row(SZ_VB), row(d, 0), row(d, 1), row(d),
                  mspec(2), mspec(4), mspec(3),
                  _const_spec(wua.shape), _const_spec(wub.shape), _const_spec(wo.shape),
                  _const_spec((1, d)), _const_spec((1, d))],
        out_specs=[row(d), row(d)],
        out_shape=[jax.ShapeDtypeStruct((t, d), F32), jax.ShapeDtypeStruct((t, d), BF16)],
        compiler_params=_params(("parallel",)),
        name="merge_ln1",
    )(ya, yb, gates, gates, x2d, mod3, mod3, mod3, wua, wub, wo, ln_g.reshape(1, d), ln_b.reshape(1, d))


def _top16(s):
    n, t = s.shape
    iota = lax.broadcasted_iota(jnp.int32, (n, t), 0)
    rank = jnp.full((n, t), float(PEER_TOPK), F32)
    work = s
    vals = []
    for a in range(PEER_TOPK):
        m = jnp.max(work, axis=0, keepdims=True)
        idx = jnp.min(jnp.where(work == m, iota, n), axis=0, keepdims=True)
        sel = iota == idx
        rank = jnp.where(sel, float(a), rank)
        work = jnp.where(sel, -jnp.inf, work)
        vals.append(m)
    return jnp.concatenate(vals, axis=0), rank


def _peer_q_kernel(h2_ref, wq_ref, sub_ref, a1_ref, lrow_ref, a2_ref, r2_ref, q_scr):
    hd = pl.program_id(1)
    kd = N_KEYS

    @pl.when(hd == 0)
    def _():
        q = jnp.dot(h2_ref[...], wq_ref[...], preferred_element_type=F32).astype(BF16)
        for h in range(PEER_HEADS):
            q_scr[h] = q[:, h * 2 * kd:(h + 1) * 2 * kd]

    q = q_scr[hd]
    s1 = lax.dot_general(sub_ref[0, 0], q[:, :kd], NT_DIMS, preferred_element_type=F32)
    s2 = lax.dot_general(sub_ref[0, 1], q[:, kd:], NT_DIMS, preferred_element_type=F32)
    v1, r1 = _top16(s1)
    v2, r2 = _top16(s2)
    half = PEER_TOPK // 2
    cand = jnp.concatenate([v1[0:1] + v2] + [v1[a:a + 1] + v2[0:half] for a in range(1, half)]
                           + [v1[half:] + v2[0:1]], axis=0)
    cv, crank = _top16(cand)
    chosen = jnp.where(crank < float(PEER_TOPK), 1.0, 0.0)
    z = jnp.sum(jnp.exp(cv - cv[0:1, :]), axis=0, keepdims=True)
    inv_z = 1.0 / z
    tail = PEER_TOPK + (half - 1) * half
    cnts = [jnp.sum(chosen[0:PEER_TOPK], axis=0, keepdims=True)]
    cnts += [jnp.sum(chosen[PEER_TOPK + (a - 1) * half:PEER_TOPK + a * half], axis=0, keepdims=True)
             for a in range(1, half)]
    cnts += [chosen[tail + a:tail + a + 1] for a in range(half)]
    lrow = jnp.zeros_like(s1)
    for a in range(PEER_TOPK):
        lrow = jnp.where(r1 == float(a), cnts[a], lrow)
    a1_ref[0] = jnp.where(r1 < float(PEER_TOPK), jnp.exp(s1 - v1[0:1, :]) * inv_z, 0.0)
    lrow_ref[0] = lrow
    a2_ref[0] = jnp.exp(s2 - v2[0:1, :])
    r2_ref[0] = r2


def _peer_q(h2, wq, sub, tm):
    t, d = h2.shape
    out = jax.ShapeDtypeStruct((PEER_HEADS, N_KEYS, t), F32)
    ospec = pl.BlockSpec((1, N_KEYS, tm), lambda i, h: (h, 0, i))
    return pl.pallas_call(
        _peer_q_kernel,
        grid=(t // tm, PEER_HEADS),
        in_specs=[pl.BlockSpec((tm, d), lambda i, h: (i, 0)),
                  _const_spec(wq.shape),
                  pl.BlockSpec((1, 2, N_KEYS, N_KEYS), lambda i, h: (h, 0, 0, 0))],
        out_specs=[ospec] * 4,
        out_shape=[out] * 4,
        scratch_shapes=[pltpu.VMEM((PEER_HEADS, tm, 2 * N_KEYS), BF16)],
        compiler_params=_params(("parallel", "arbitrary")),
        name="peer_select",
    )(h2, wq, sub)


def _peer_kernel(h2_ref, a1_ref, lrow_ref, a2_ref, r2_ref, u_ref, v_ref, x1_ref, g2_ref, lg_ref, lb_ref,
                 o_ref, acc_scr, w_scr, xu_scr, g_scr, *, rows, nj):
    n = pl.program_id(0)
    jp = jnp.maximum(n - 1, 0) % nj

    @pl.when(n == 0)
    def _():
        w_scr[...] = jnp.zeros(w_scr.shape, BF16)

    @pl.when(jp == 0)
    def _():
        acc_scr[...] = jnp.zeros(acc_scr.shape, F32)

    acc_scr[...] += jnp.dot(w_scr[...], v_ref[...], preferred_element_type=F32)

    tm = h2_ref.shape[0]
    for tc in range(tm // LANES):
        cs = slice(tc * LANES, (tc + 1) * LANES)
        for sb in range(N_KEYS // GATE_SUB):
            ss = slice(sb * GATE_SUB, (sb + 1) * GATE_SUB)
            g = [None] * rows
            for h in range(PEER_HEADS):
                r2 = r2_ref[h, ss, cs]
                a2 = a2_ref[h, ss, cs]
                for r in range(rows):
                    term = jnp.where(r2 < lrow_ref[h, r:r + 1, cs], a1_ref[h, r:r + 1, cs] * a2, 0.0)
                    g[r] = term if g[r] is None else g[r] + term
            for r in range(rows):
                g_scr[r * N_KEYS + sb * GATE_SUB:r * N_KEYS + (sb + 1) * GATE_SUB, cs] = g[r]

    xu_scr[...] = lax.dot_general(u_ref[...], h2_ref[...], NT_DIMS, preferred_element_type=F32)
    for r in range(rows):
        rs = slice(r * N_KEYS, (r + 1) * N_KEYS)
        for tc in range(tm // LANES):
            cs = slice(tc * LANES, (tc + 1) * LANES)
            xu = xu_scr[rs, cs]
            w_t = 0.5 * xu * (1.0 + lax.erf(xu * (2.0 ** -0.5))) * g_scr[rs, cs]
            w_scr[cs, rs] = w_t.T.astype(BF16)

    @pl.when(jnp.logical_and(n > 0, jp == nj - 1))
    def _():
        r = DN_ALPHA * x1_ref[...] + g2_ref[0] * acc_scr[...]
        o_ref[...] = _layer_norm(r, lg_ref[...], lb_ref[...])


def _peer(h2, sel, u, v, x1, mod3, boff, seq, ln_g, ln_b, tm, te):
    t, d = h2.shape
    rows = te // N_KEYS
    bpb = seq // tm
    nj = N_EXPERTS // te
    a1, lrow, a2, r2 = sel
    ni = t // tm
    last = ni * nj - 1
    once = dict(pipeline_mode=pl.Buffered(1))
    cur = lambda n: jnp.minimum(n, last)
    prv = lambda n: jnp.maximum(n - 1, 0)
    full = pl.BlockSpec((PEER_HEADS, N_KEYS, tm), lambda n: (0, 0, cur(n) // nj), **once)
    part = pl.BlockSpec((PEER_HEADS, rows, tm), lambda n: (0, cur(n) % nj, cur(n) // nj))
    return pl.pallas_call(
        functools.partial(_peer_kernel, rows=rows, nj=nj),
        grid=(ni * nj + 1,),
        in_specs=[pl.BlockSpec((tm, d), lambda n: (cur(n) // nj, 0), **once), part, part, full, full,
                  pl.BlockSpec((te, d), lambda n: (cur(n) % nj, 0)),
                  pl.BlockSpec((te, d), lambda n: (prv(n) % nj, 0)),
                  pl.BlockSpec((tm, d), lambda n: (prv(n) // nj, 0), **once),
                  pl.BlockSpec((1, 1, d), lambda n: (boff + prv(n) // nj // bpb, 0, 5)),
                  _const_spec((1, d)), _const_spec((1, d))],
        out_specs=pl.BlockSpec((tm, d), lambda n: (prv(n) // nj, 0)),
        out_shape=jax.ShapeDtypeStruct((t, d), F32),
        scratch_shapes=[pltpu.VMEM((tm, d), F32), pltpu.VMEM((tm, te), BF16),
                        pltpu.VMEM((te, tm), F32), pltpu.VMEM((te, tm), F32)],
        compiler_params=_params(("arbitrary",)),
        name="peer_experts",
    )(h2, a1, lrow, a2, r2, u, v, x1, mod3, ln_g.reshape(1, d), ln_b.reshape(1, d))


def _trunk(x, mod3, boff, w):
    nbatch, seq, d = x.shape
    x2d = x.reshape(nbatch * seq, d)
    qkv = _proj(x2d, mod3, boff, seq, w["w_qkv"], 512, 1536, BF16, False)
    gates = _proj(x2d, mod3, boff, seq, w["w_gate"], 512, 1024, F32, True)
    ya = _window(qkv, w["sink"], nbatch, seq)
    t = nbatch * seq
    vt = qkv[:, SZ_QA + 2 * SZ_QB:SZ_QA + 2 * SZ_QB + SZ_VB].T.reshape(B_HEADS, 128, t)
    vt_aug = jnp.concatenate([vt, jnp.ones((B_HEADS, DIFF_AUG, t), BF16)], axis=1)
    yb = _diff(qkv, vt_aug, w["lam_q1"], w["lam_k1"], w["lam_q2"], w["lam_k2"], w["norm_g"], nbatch, seq, 256, 1024)
    x1, h2 = _merge(ya, yb, gates, x2d, mod3, boff, seq, w["w_up_a"], w["w_up_b"], w["w_o"],
                    w["ln1_g"], w["ln1_b"], 256)
    sel = _peer_q(h2, w["peer_wq"], w["peer_sub"], 512)
    y = _peer(h2, sel, w["peer_u"], w["peer_v"], x1, mod3, boff, seq, w["ln2_g"], w["ln2_b"], 512, 1024)
    return y.reshape(nbatch, seq, d)


def kernel(x_prompt, x_sample, c_prompt, c_sample, w_ada, b_ada, w_in, sink_a, lam_q1, lam_k1, lam_q2, lam_k2, diff_norm_g, w_up_a, w_up_b, w_o, ln1_g, ln1_b, peer_wq, peer_subkeys, peer_u, peer_v, ln2_g, ln2_b):
    layer = 0
    nbp, nbs = x_prompt.shape[0], x_sample.shape[0]
    pad = (-(nbp + nbs)) % 8
    c_all = jnp.concatenate([c_prompt, c_sample, jnp.zeros((pad, D_MODEL), F32)], axis=0)
    mod = _ada(c_all, w_ada[layer], b_ada[layer])
    mod3 = mod.reshape(mod.shape[0], 1, 6 * D_MODEL)

    wi = w_in[layer]
    o_ka = SZ_QA
    o_va = o_ka + SZ_KA
    o_qb = o_va + SZ_KA
    o_kb = o_qb + SZ_QB
    o_vb = o_kb + SZ_QB
    o_g = o_vb + SZ_VB
    w_qkv = jnp.concatenate([wi[:, :o_ka], wi[:, o_qb:o_g], wi[:, o_ka:o_qb]], axis=1).astype(BF16)
    weights = dict(
        w_qkv=w_qkv, w_gate=wi[:, o_g:].astype(BF16), sink=sink_a[layer].astype(F32),
        lam_q1=lam_q1[layer], lam_k1=lam_k1[layer], lam_q2=lam_q2[layer], lam_k2=lam_k2[layer],
        norm_g=diff_norm_g[layer],
        w_up_a=w_up_a[layer].astype(BF16), w_up_b=w_up_b[layer].astype(BF16), w_o=w_o[layer].astype(BF16),
        ln1_g=ln1_g[layer], ln1_b=ln1_b[layer],
        peer_wq=peer_wq[layer].astype(BF16), peer_sub=peer_subkeys[layer].astype(BF16),
        peer_u=peer_u[layer].astype(BF16), peer_v=peer_v[layer].astype(BF16),
        ln2_g=ln2_g[layer], ln2_b=ln2_b[layer],
    )
    y_prompt = _trunk(x_prompt, mod3, 0, weights)
    y_sample = _trunk(x_sample, mod3, nbp, weights)
    return (y_prompt, y_sample)
```

```python
import functools
import math

import jax
import jax.numpy as jnp
from jax import lax
from jax.experimental import pallas as pl
from jax.experimental.pallas import tpu as pltpu

F32 = jnp.float32
BF16 = jnp.bfloat16

D_MODEL = 2048
HEAD_DIM = 128
A_Q_HEADS = 8
A_KV_HEADS = 2
A_GROUP = A_Q_HEADS // A_KV_HEADS
WINDOW = 128
B_HEADS = 8
B_QK_DIM = 64
PEER_HEADS = 8
N_KEYS = 128
N_EXPERTS = N_KEYS * N_KEYS
PEER_TOPK = 16
SZ_QA = A_Q_HEADS * HEAD_DIM
SZ_KA = A_KV_HEADS * HEAD_DIM
SZ_QB = B_HEADS * 2 * B_QK_DIM
SZ_VB = B_HEADS * 2 * B_QK_DIM
DEPTH = 1
DN_ALPHA = (2.0 * DEPTH) ** 0.25
LN_EPS = 1e-5
RMS_EPS = 1e-5
LAM_INIT = 0.8 - 0.6 * math.exp(-0.3 * 0)

VMEM_LIMIT_V7X = 56 * 1024 * 1024

NT_DIMS = (((1,), (1,)), ((), ()))
TN_DIMS = (((0,), (0,)), ((), ()))


LANES = 128
GATE_SUB = 32


def _params(sem):
    return pltpu.CompilerParams(dimension_semantics=sem, vmem_limit_bytes=VMEM_LIMIT_V7X)


def _const_spec(shape):
    nd = len(shape)
    return pl.BlockSpec(shape, lambda *_: (0,) * nd, pipeline_mode=pl.Buffered(1))


def _layer_norm(r, g, b):
    mu = jnp.mean(r, axis=-1, keepdims=True)
    c = r - mu
    var = jnp.mean(c * c, axis=-1, keepdims=True)
    return c * lax.rsqrt(var + LN_EPS) * g + b


def _ada_kernel(c_ref, w_ref, b_ref, o_ref):
    c = c_ref[...]
    a = (c * jax.nn.sigmoid(c)).astype(BF16)
    o_ref[...] = jnp.dot(a, w_ref[...].astype(BF16), preferred_element_type=F32) + b_ref[...]


def _ada(c_all, w_ada, b_ada):
    nb, d = c_all.shape
    n = w_ada.shape[1]
    tn = 1024
    return pl.pallas_call(
        _ada_kernel,
        grid=(n // tn,),
        in_specs=[pl.BlockSpec((nb, d), lambda j: (0, 0)),
                  pl.BlockSpec((d, tn), lambda j: (0, j)),
                  pl.BlockSpec((1, tn), lambda j: (0, j))],
        out_specs=pl.BlockSpec((nb, tn), lambda j: (0, j)),
        out_shape=jax.ShapeDtypeStruct((nb, n), F32),
        compiler_params=_params(("arbitrary",)),
        name="ada",
    )(c_all, w_ada, b_ada.reshape(1, n))


def _proj_kernel(x_ref, sc_ref, sh_ref, w_ref, o_ref, h_scr, *, gate):
    @pl.when(pl.program_id(1) == 0)
    def _():
        h = x_ref[...] * (1.0 + sc_ref[0]) + sh_ref[0]
        h_scr[...] = h.astype(BF16)

    acc = jnp.dot(h_scr[...], w_ref[...], preferred_element_type=F32)
    if gate:
        acc = jax.nn.sigmoid(acc)
    o_ref[...] = acc.astype(o_ref.dtype)


def _proj(x2d, mod3, boff, seq, w, tm, tn, out_dtype, gate):
    t, d = x2d.shape
    n = w.shape[1]
    bpb = seq // tm
    return pl.pallas_call(
        functools.partial(_proj_kernel, gate=gate),
        grid=(t // tm, n // tn),
        in_specs=[pl.BlockSpec((tm, d), lambda i, j: (i, 0)),
                  pl.BlockSpec((1, 1, d), lambda i, j: (boff + i // bpb, 0, 1)),
                  pl.BlockSpec((1, 1, d), lambda i, j: (boff + i // bpb, 0, 0)),
                  pl.BlockSpec((d, tn), lambda i, j: (0, j))],
        out_specs=pl.BlockSpec((tm, tn), lambda i, j: (i, j)),
        out_shape=jax.ShapeDtypeStruct((t, n), out_dtype),
        scratch_shapes=[pltpu.VMEM((tm, d), BF16)],
        compiler_params=_params(("parallel", "arbitrary")),
        name="proj_gate" if gate else "proj_qkv",
    )(x2d, mod3, mod3, w)


def _window_kernel(sink_ref, q_ref, kp_ref, kc_ref, kn_ref, vp_ref, vc_ref, vn_ref, o_ref, *, seq):
    blk = WINDOW
    n = pl.program_id(1)
    krel = lax.broadcasted_iota(jnp.int32, (3 * blk, blk), 0) - blk
    qrel = lax.broadcasted_iota(jnp.int32, (3 * blk, blk), 1)
    dist_i = jnp.abs(krel - qrel)
    kabs = n * blk + krel
    valid = (dist_i <= WINDOW) & (kabs >= 0) & (kabs < seq)
    dist = dist_i.astype(F32)
    scale = HEAD_DIM ** -0.5
    for g in range(A_KV_HEADS):
        cs = slice(g * HEAD_DIM, (g + 1) * HEAD_DIM)
        k = jnp.concatenate([kp_ref[:, cs], kc_ref[:, cs], kn_ref[:, cs]], axis=0)
        v = jnp.concatenate([vp_ref[:, cs], vc_ref[:, cs], vn_ref[:, cs]], axis=0)
        for j in range(A_GROUP):
            h = g * A_GROUP + j
            slope = 2.0 ** (-8.0 * (h + 1) / A_Q_HEADS)
            hs = slice(h * HEAD_DIM, (h + 1) * HEAD_DIM)
            q = q_ref[:, hs]
            s = lax.dot_general(k, q, NT_DIMS, preferred_element_type=F32) * scale
            s = jnp.where(valid, s - slope * dist, -jnp.inf)
            sink = sink_ref[h]
            m = jnp.maximum(jnp.max(s, axis=0, keepdims=True), sink)
            p = jnp.exp(s - m)
            l = jnp.sum(p, axis=0, keepdims=True) + jnp.exp(sink - m)
            o_t = lax.dot_general(v, p.astype(BF16), TN_DIMS, preferred_element_type=F32)
            o_t = o_t * (1.0 / l)
            o_ref[:, hs] = o_t.T.astype(o_ref.dtype)


def _window(qkv, sink, nbatch, seq):
    blk = WINDOW
    nb = seq // blk
    t = nbatch * seq
    kcol = (SZ_QA + SZ_QB + 2 * SZ_VB) // (A_KV_HEADS * HEAD_DIM)
    vcol = kcol + 1

    def row(b, n, off):
        return b * nb + jnp.clip(n + off, 0, nb - 1)

    kv_specs = [pl.BlockSpec((blk, SZ_KA), functools.partial(lambda b, n, col, off: (row(b, n, off), col), col=col, off=off))
                for col in (kcol, vcol) for off in (-1, 0, 1)]
    return pl.pallas_call(
        functools.partial(_window_kernel, seq=seq),
        grid=(nbatch, nb),
        in_specs=[pl.BlockSpec(memory_space=pltpu.SMEM),
                  pl.BlockSpec((blk, SZ_QA), lambda b, n: (b * nb + n, 0))] + kv_specs,
        out_specs=pl.BlockSpec((blk, SZ_QA), lambda b, n: (b * nb + n, 0)),
        out_shape=jax.ShapeDtypeStruct((t, SZ_QA), BF16),
        compiler_params=_params(("parallel", "parallel")),
        name="window_attn",
    )(sink, qkv, qkv, qkv, qkv, qkv, qkv, qkv)


DIFF_AUG = 16
DIFF_SPLIT = 256
DIFF_QK_AHEAD = 1
DIFF_PV_BEHIND = 2
DIFF_ROWS = 128
DIFF_UNDERFLOW = 160.0
DIFF_NORM_SLACK = 1.05
DIFF_FIRST_HEADS = (0, 2, 3, 4, 5)
LOG2E = math.log2(math.e)
LOG2E_PARTS = (1.4453125, -0.00262451171875, 7.063150405883789e-06)


def _diff_step(q_bias_rows, group_shift, k_ref, kc_ref, vt_ref, qt_scr, m_scr, acc_scr, s_scr, p_scr, dist, *,
               tq, tk, on_diag, first_head=0):
    nhm = 2 * B_HEADS
    ns, npb = DIFF_QK_AHEAD + 1, DIFF_PV_BEHIND + 1
    ngrp = tk // DIFF_SPLIT
    tiles = DIFF_SPLIT // DIFF_ROWS

    def qk(hm):
        h = hm // 2
        k_aug = jnp.concatenate([k_ref[:, h * 128:(h + 1) * 128], kc_ref[...]], axis=1)
        s_scr[hm % ns] = jnp.dot(k_aug, qt_scr[hm], preferred_element_type=F32)

    def softmax(hm):
        h, slot, pslot = hm // 2, hm % ns, hm % npb
        slope = 2.0 ** (-8.0 * (h + 1) / B_HEADS)
        shifts = [group_shift(h, g) for g in range(ngrp)]
        alphas = []
        for c in range(tq // LANES):
            cs = slice(c * LANES, (c + 1) * LANES)
            m_blk = None
            for g in range(ngrp):
                mrun = None
                for rb in range(tiles):
                    rs = slice(g * DIFF_SPLIT + rb * DIFF_ROWS, g * DIFF_SPLIT + (rb + 1) * DIFF_ROWS)
                    s = s_scr[slot, rs, cs]
                    if on_diag:
                        s = s - (slope * LOG2E) * dist[rs, cs]
                        s_scr[slot, rs, cs] = s
                    mrun = s if mrun is None else jnp.maximum(mrun, s)
                mg = jnp.max(mrun, axis=0, keepdims=True)
                if shifts[g] is not None:
                    mg = mg + shifts[g]
                m_blk = mg if m_blk is None else jnp.maximum(m_blk, mg)
            m_old = m_scr[hm:hm + 1, cs]
            m_new = jnp.maximum(m_old, m_blk)
            alphas.append(jnp.exp2(m_old - m_new))
            for g in range(ngrp):
                m_g = m_new if shifts[g] is None else m_new - shifts[g]
                for rb in range(tiles):
                    rs = slice(g * DIFF_SPLIT + rb * DIFF_ROWS, g * DIFF_SPLIT + (rb + 1) * DIFF_ROWS)
                    p_scr[pslot, rs, cs] = jnp.exp2(s_scr[slot, rs, cs] - m_g).astype(BF16)
            m_scr[hm:hm + 1, cs] = m_new
        return jnp.concatenate(alphas, axis=1)

    def pv(hm, alpha):
        r = jnp.dot(vt_ref[hm // 2], p_scr[hm % npb], preferred_element_type=F32)
        acc_scr[hm] = alpha * acc_scr[hm] + r

    lo = 2 * first_head
    for hm in range(lo, nhm):
        qt_scr[hm, 128:128 + DIFF_AUG, :] = q_bias_rows(hm // 2)
    for hm in range(lo, lo + DIFF_QK_AHEAD):
        qk(hm)
    alpha = {}
    for t in range(lo, nhm + DIFF_PV_BEHIND):
        if t + DIFF_QK_AHEAD < nhm:
            qk(t + DIFF_QK_AHEAD)
        if t < nhm:
            alpha[t] = softmax(t)
        if t - DIFF_PV_BEHIND >= lo:
            pv(t - DIFF_PV_BEHIND, alpha.pop(t - DIFF_PV_BEHIND))


def _diff_kernel(qn_ref, kn_ref, lq1_ref, lk1_ref, lq2_ref, lk2_ref, ng_ref, q_ref, k_ref, kc_ref, vt_ref, o_ref,
                 qt_scr, m_scr, acc_scr, s_scr, p_scr, dist_scr, *, tq, tk):
    b = pl.program_id(0)
    qi = pl.program_id(1)
    kv = pl.program_id(2)
    nkv = pl.num_programs(2)
    nq = pl.num_programs(1)
    kd = (qi * tq) // tk
    kb = (kd + kv) % nkv

    @pl.when(kv == 0)
    def _():
        m_scr[...] = jnp.full(m_scr.shape, -jnp.inf, F32)
        acc_scr[...] = jnp.zeros(acc_scr.shape, F32)
        row = lax.broadcasted_iota(jnp.int32, (2 * B_QK_DIM, tq), 0)
        zeros = jnp.zeros((128, tq), BF16)
        for h in range(B_HEADS):
            q_t = (q_ref[:, h * 128:(h + 1) * 128].astype(F32) * (B_QK_DIM ** -0.5 * LOG2E)).T
            qt_scr[2 * h, 0:128, :] = jnp.where(row < B_QK_DIM, q_t, 0.0).astype(BF16)
            qt_scr[2 * h + 1, 0:128, :] = jnp.where(row >= B_QK_DIM, q_t, 0.0).astype(BF16)
            qt_scr[2 * h, 128:256, :] = zeros
            qt_scr[2 * h + 1, 128:256, :] = zeros

    q0 = qi * tq
    k0 = kb * tk
    left = k0 + tk <= q0
    right = k0 >= q0 + tq
    args = (k_ref, kc_ref, vt_ref, qt_scr, m_scr, acc_scr, s_scr, p_scr)

    @pl.when(jnp.logical_or(left, right))
    def _():
        sgn = jnp.where(left, 1.0, -1.0).astype(F32)
        off = (q0 - k0).astype(F32)
        rid = lax.broadcasted_iota(jnp.int32, (DIFF_AUG, tq), 0)
        il = (lax.broadcasted_iota(jnp.int32, (DIFF_AUG, tq), 1) - DIFF_SPLIT // 2).astype(F32)

        def rows(h):
            g = sgn * 2.0 ** (-8.0 * (h + 1) / B_HEADS)
            r = jnp.where(rid < 3, -g * il, 0.0)
            for part, c in enumerate(LOG2E_PARTS):
                r = jnp.where(rid == 3 + part, g * c, r)
            return r.astype(BF16)

        def shift(h, grp):
            return -(sgn * 2.0 ** (-8.0 * (h + 1) / B_HEADS) * LOG2E) * (off - float(DIFF_SPLIT * grp))

        dmin = jnp.where(left, q0 - (k0 + tk - 1), k0 - (q0 + tq - 1)).astype(F32)
        nhm = 2 * B_HEADS
        qoff = (b * nq + qi) * nhm
        koff = (b * nkv + kb) * nhm
        doff = (b * nkv + kd) * nhm
        scale = DIFF_NORM_SLACK * (B_QK_DIM ** -0.5) * LOG2E
        lead = jnp.int32(0)
        ok = jnp.bool_(True)
        for h in range(B_HEADS):
            bound = jnp.float32(0.0)
            for mp in range(2):
                hm = 2 * h + mp
                bound = jnp.maximum(bound, qn_ref[qoff + hm] * (kn_ref[koff + hm] + kn_ref[doff + hm]))
            slope = 2.0 ** (-8.0 * (h + 1) / B_HEADS)
            ok = jnp.logical_and(ok, scale * bound - slope * LOG2E * dmin < -DIFF_UNDERFLOW)
            lead = lead + ok.astype(jnp.int32)
        for idx, first in enumerate(DIFF_FIRST_HEADS):
            upper = DIFF_FIRST_HEADS[idx + 1] if idx + 1 < len(DIFF_FIRST_HEADS) else B_HEADS + 1

            @pl.when(jnp.logical_and(lead >= first, lead < upper))
            def _(first=first):
                _diff_step(rows, shift, *args, None, tq=tq, tk=tk, on_diag=False, first_head=first)

    @pl.when(jnp.logical_not(jnp.logical_or(left, right)))
    def _():
        kpos = k0 + lax.broadcasted_iota(jnp.int32, (tk, tq), 0)
        qpos = q0 + lax.broadcasted_iota(jnp.int32, (tk, tq), 1)
        dist_scr[...] = jnp.abs(kpos - qpos).astype(F32)
        _diff_step(lambda h: jnp.zeros((DIFF_AUG, tq), BF16), lambda h, g: None, *args, dist_scr,
                   tq=tq, tk=tk, on_diag=True)

    @pl.when(kv == nkv - 1)
    def _():
        lam = (jnp.exp(jnp.sum(lq1_ref[...] * lk1_ref[...], axis=-1, keepdims=True))
               - jnp.exp(jnp.sum(lq2_ref[...] * lk2_ref[...], axis=-1, keepdims=True)) + LAM_INIT)
        for h in range(B_HEADS):
            a0, a1 = acc_scr[2 * h], acc_scr[2 * h + 1]
            o0 = a0[0:128] * (1.0 / a0[128:129])
            o1 = a1[0:128] * (1.0 / a1[128:129])
            o = (o0 - lam * o1).T
            o = o * lax.rsqrt(jnp.mean(o * o, axis=-1, keepdims=True) + RMS_EPS) * ng_ref[...]
            o_ref[:, h * 128:(h + 1) * 128] = (o * (1.0 - LAM_INIT)).astype(o_ref.dtype)


def _rownorm_kernel(x_ref, ind_ref, o_ref):
    x = x_ref[...].astype(F32)
    ss = jnp.dot(x * x, ind_ref[...], preferred_element_type=F32)
    o_ref[0] = jnp.broadcast_to(jnp.sqrt(jnp.max(ss, axis=0, keepdims=True)), o_ref.shape[1:])


def _rownorm_max(qkv, col, rows):
    t = qkv.shape[0]
    ind = (jnp.arange(SZ_QB)[:, None] // B_QK_DIM == jnp.arange(LANES)[None, :]).astype(F32)
    out = pl.pallas_call(
        _rownorm_kernel,
        grid=(t // rows,),
        in_specs=[pl.BlockSpec((rows, SZ_QB), lambda i: (i, col)), _const_spec(ind.shape)],
        out_specs=pl.BlockSpec((1, 8, LANES), lambda i: (i, 0, 0)),
        out_shape=jax.ShapeDtypeStruct((t // rows, 8, LANES), F32),
        compiler_params=_params(("parallel",)),
        name="diff_rownorm",
    )(qkv, ind)
    return out[:, 0, :2 * B_HEADS].reshape(-1)


def _diff(qkv, vt_aug, lam_q1, lam_k1, lam_q2, lam_k2, norm_g, nbatch, seq, tq, tk):
    t = nbatch * seq
    nq, nk = seq // tq, seq // tk
    qn = _rownorm_max(qkv, 1, tq)
    kn = _rownorm_max(qkv, 2, tk)
    kblk = lambda i, j: ((i * tq) // tk + j) % nk
    w = SZ_QB
    nhm = 2 * B_HEADS
    small = lambda a: a.reshape(1, -1).astype(F32)
    sspec = lambda n: pl.BlockSpec((1, n), lambda b, i, j: (0, 0))
    assert tq == DIFF_SPLIT and tk % DIFF_SPLIT == 0
    jl = (jnp.arange(tk) % DIFF_SPLIT - DIFF_SPLIT // 2).astype(F32)
    kc = jnp.zeros((tk, 128), F32).at[:, 0:3].set(jnp.asarray(LOG2E_PARTS, F32)).at[:, 3:6].set(jl[:, None])
    return pl.pallas_call(
        functools.partial(_diff_kernel, tq=tq, tk=tk),
        grid=(nbatch, nq, nk),
        in_specs=[pl.BlockSpec(memory_space=pltpu.SMEM)] * 2 + [sspec(B_QK_DIM)] * 4 + [sspec(2 * B_QK_DIM),
                  pl.BlockSpec((tq, w), lambda b, i, j: (b * nq + i, 1)),
                  pl.BlockSpec((tk, w), lambda b, i, j: (b * nk + kblk(i, j), 2)),
                  _const_spec((tk, 128)),
                  pl.BlockSpec((B_HEADS, 128 + DIFF_AUG, tk), lambda b, i, j: (0, 0, b * nk + kblk(i, j)))],
        out_specs=pl.BlockSpec((tq, w), lambda b, i, j: (b * nq + i, 0)),
        out_shape=jax.ShapeDtypeStruct((t, w), BF16),
        scratch_shapes=[pltpu.VMEM((nhm, 256, tq), BF16),
                        pltpu.VMEM((nhm, tq), F32),
                        pltpu.VMEM((nhm, 128 + DIFF_AUG, tq), F32),
                        pltpu.VMEM((DIFF_QK_AHEAD + 1, tk, tq), F32),
                        pltpu.VMEM((DIFF_PV_BEHIND + 1, tk, tq), BF16),
                        pltpu.VMEM((tk, tq), F32)],
        compiler_params=_params(("parallel", "parallel", "arbitrary")),
        name="diff_attn",
    )(qn, kn, small(lam_q1), small(lam_k1), small(lam_q2), small(lam_k2), small(norm_g), qkv, qkv, kc.astype(BF16), vt_aug)


def _merge_kernel(ya_ref, yb_ref, ga_ref, gb_ref, x_ref, g1_ref, sc2_ref, sh2_ref,
                  wua_ref, wub_ref, wo_ref, lg_ref, lb_ref, x1_ref, h2_ref):
    ua = jnp.dot(ya_ref[...], wua_ref[...], preferred_element_type=F32)
    ub = jnp.dot(yb_ref[...], wub_ref[...], preferred_element_type=F32)
    merged = ga_ref[...] * ua + gb_ref[...] * ub
    z = jnp.dot(merged.astype(BF16), wo_ref[...], preferred_element_type=F32)
    r = DN_ALPHA * x_ref[...] + g1_ref[0] * z
    x1 = _layer_norm(r, lg_ref[...], lb_ref[...])
    x1_ref[...] = x1
    h2_ref[...] = (x1 * (1.0 + sc2_ref[0]) + sh2_ref[0]).astype(BF16)


def _merge(ya, yb, gates, x2d, mod3, boff, seq, wua, wub, wo, ln_g, ln_b, tm):
    t, d = x2d.shape
    bpb = seq // tm
    mspec = lambda c: pl.BlockSpec((1, 1, d), lambda i: (boff + i // bpb, 0, c))
    row = lambda n, c=0: pl.BlockSpec((tm, n), lambda i: (i, c))
    return pl.pallas_call(
        _merge_kernel,
        grid=(t // tm,),
        in_specs=[row(SZ_QA), row(SZ_VB), row(d, 0), row(d, 1), row(d),
                  mspec(2), mspec(4), mspec(3),
                  _const_spec(wua.shape), _const_spec(wub.shape), _const_spec(wo.shape),
                  _const_spec((1, d)), _const_spec((1, d))],
        out_specs=[row(d), row(d)],
        out_shape=[jax.ShapeDtypeStruct((t, d), F32), jax.ShapeDtypeStruct((t, d), BF16)],
        compiler_params=_params(("parallel",)),
        name="merge_ln1",
    )(ya, yb, gates, gates, x2d, mod3, mod3, mod3, wua, wub, wo, ln_g.reshape(1, d), ln_b.reshape(1, d))


def _top16(s, ties):
    n, t = s.shape
    iota = lax.broadcasted_iota(jnp.int32, (n, t), 0)
    rank = jnp.full((n, t), float(PEER_TOPK), F32)
    work = s
    vals = []
    for a in range(PEER_TOPK):
        m = jnp.max(work, axis=0, keepdims=True)
        sel = work == m
        if ties:
            idx = jnp.min(jnp.where(sel, iota, n), axis=0, keepdims=True)
            sel = iota == idx
        rank = jnp.where(sel, float(a), rank)
        work = jnp.where(sel, -jnp.inf, work)
        vals.append(m)
    bad = None
    if not ties:
        removed = jnp.sum(jnp.where(rank < float(PEER_TOPK), 1.0, 0.0), axis=0, keepdims=True)
        bad = jnp.where(removed != float(PEER_TOPK), 1.0, 0.0)
    return jnp.concatenate(vals, axis=0), rank, bad


def _peer_select(s1, s2, ties):
    v1, r1, bad1 = _top16(s1, ties)
    v2, r2, bad2 = _top16(s2, ties)
    half = PEER_TOPK // 2
    cand = jnp.concatenate([v1[0:1] + v2] + [v1[a:a + 1] + v2[0:half] for a in range(1, half)]
                           + [v1[half:] + v2[0:1]], axis=0)
    cv, crank, bad3 = _top16(cand, ties)
    chosen = jnp.where(crank < float(PEER_TOPK), 1.0, 0.0)
    z = jnp.sum(jnp.exp(cv - cv[0:1, :]), axis=0, keepdims=True)
    inv_z = 0.5 / z
    tail = PEER_TOPK + (half - 1) * half
    cnts = [jnp.sum(chosen[0:PEER_TOPK], axis=0, keepdims=True)]
    cnts += [jnp.sum(chosen[PEER_TOPK + (a - 1) * half:PEER_TOPK + a * half], axis=0, keepdims=True)
             for a in range(1, half)]
    cnts += [chosen[tail + a:tail + a + 1] for a in range(half)]
    lrow = jnp.zeros_like(s1)
    for a in range(PEER_TOPK):
        lrow = jnp.where(r1 == float(a), cnts[a], lrow)
    a1 = jnp.where(r1 < float(PEER_TOPK), jnp.exp(s1 - v1[0:1, :]) * inv_z, 0.0)
    a2 = jnp.exp(s2 - v2[0:1, :])
    n_bad = None if ties else jnp.sum(bad1 + bad2 + bad3)
    return (a1, lrow, a2, r2), n_bad


def _peer_q_kernel(h2_ref, wq_ref, sub_ref, a1_ref, lrow_ref, a2_ref, r2_ref, q_scr):
    hd = pl.program_id(1)
    kd = N_KEYS
    out_refs = (a1_ref, lrow_ref, a2_ref, r2_ref)

    @pl.when(hd == 0)
    def _():
        q = jnp.dot(h2_ref[...], wq_ref[...], preferred_element_type=F32).astype(BF16)
        for h in range(PEER_HEADS):
            q_scr[h] = q[:, h * 2 * kd:(h + 1) * 2 * kd]

    q = q_scr[hd]
    s1 = lax.dot_general(sub_ref[0, 0], q[:, :kd], NT_DIMS, preferred_element_type=F32)
    s2 = lax.dot_general(sub_ref[0, 1], q[:, kd:], NT_DIMS, preferred_element_type=F32)
    outs, n_bad = _peer_select(s1, s2, ties=False)
    for ref, val in zip(out_refs, outs):
        ref[0] = val

    @pl.when(n_bad > 0.0)
    def _():
        outs, _ = _peer_select(s1, s2, ties=True)
        for ref, val in zip(out_refs, outs):
            ref[0] = val


def _peer_q(h2, wq, sub, tm):
    t, d = h2.shape
    out = jax.ShapeDtypeStruct((PEER_HEADS, N_KEYS, t), F32)
    ospec = pl.BlockSpec((1, N_KEYS, tm), lambda i, h: (h, 0, i))
    return pl.pallas_call(
        _peer_q_kernel,
        grid=(t // tm, PEER_HEADS),
        in_specs=[pl.BlockSpec((tm, d), lambda i, h: (i, 0)),
                  _const_spec(wq.shape),
                  pl.BlockSpec((1, 2, N_KEYS, N_KEYS), lambda i, h: (h, 0, 0, 0))],
        out_specs=[ospec] * 4,
        out_shape=[out] * 4,
        scratch_shapes=[pltpu.VMEM((PEER_HEADS, tm, 2 * N_KEYS), BF16)],
        compiler_params=_params(("parallel", "arbitrary")),
        name="peer_select",
    )(h2, wq, sub)


def _peer_kernel(h2_ref, a1_ref, lrow_ref, a2_ref, r2_ref, u_ref, v_ref, x1_ref, g2_ref, lg_ref, lb_ref,
                 o_ref, acc_scr, w_scr, xu_scr, g_scr, *, rows, nj):
    n = pl.program_id(0)
    jp = jnp.maximum(n - 1, 0) % nj

    @pl.when(n == 0)
    def _():
        w_scr[...] = jnp.zeros(w_scr.shape, BF16)

    @pl.when(jp == 0)
    def _():
        acc_scr[...] = jnp.zeros(acc_scr.shape, F32)

    acc_scr[...] += jnp.dot(w_scr[...], v_ref[...], preferred_element_type=F32)

    tm = h2_ref.shape[0]
    for tc in range(tm // LANES):
        cs = slice(tc * LANES, (tc + 1) * LANES)
        for sb in range(N_KEYS // GATE_SUB):
            ss = slice(sb * GATE_SUB, (sb + 1) * GATE_SUB)
            g = [None] * rows
            for h in range(PEER_HEADS):
                r2 = r2_ref[h, ss, cs]
                a2 = a2_ref[h, ss, cs]
                for r in range(rows):
                    term = jnp.where(r2 < lrow_ref[h, r:r + 1, cs], a1_ref[h, r:r + 1, cs] * a2, 0.0)
                    g[r] = term if g[r] is None else g[r] + term
            for r in range(rows):
                g_scr[r * N_KEYS + sb * GATE_SUB:r * N_KEYS + (sb + 1) * GATE_SUB, cs] = g[r]

    xu_scr[...] = lax.dot_general(u_ref[...], h2_ref[...], NT_DIMS, preferred_element_type=F32)
    for r in range(rows):
        rs = slice(r * N_KEYS, (r + 1) * N_KEYS)
        for tc in range(tm // LANES):
            cs = slice(tc * LANES, (tc + 1) * LANES)
            xu = xu_scr[rs, cs]
            w_t = xu * (1.0 + lax.erf(xu * (2.0 ** -0.5))) * g_scr[rs, cs]
            w_scr[cs, rs] = w_t.T.astype(BF16)

    @pl.when(jnp.logical_and(n > 0, jp == nj - 1))
    def _():
        r = DN_ALPHA * x1_ref[...] + g2_ref[0] * acc_scr[...]
        o_ref[...] = _layer_norm(r, lg_ref[...], lb_ref[...])


def _peer(h2, sel, u, v, x1, mod3, boff, seq, ln_g, ln_b, tm, te):
    t, d = h2.shape
    rows = te // N_KEYS
    bpb = seq // tm
    nj = N_EXPERTS // te
    a1, lrow, a2, r2 = sel
    ni = t // tm
    last = ni * nj - 1
    once = dict(pipeline_mode=pl.Buffered(1))
    cur = lambda n: jnp.minimum(n, last)
    prv = lambda n: jnp.maximum(n - 1, 0)
    full = pl.BlockSpec((PEER_HEADS, N_KEYS, tm), lambda n: (0, 0, cur(n) // nj), **once)
    part = pl.BlockSpec((PEER_HEADS, rows, tm), lambda n: (0, cur(n) % nj, cur(n) // nj))
    return pl.pallas_call(
        functools.partial(_peer_kernel, rows=rows, nj=nj),
        grid=(ni * nj + 1,),
        in_specs=[pl.BlockSpec((tm, d), lambda n: (cur(n) // nj, 0), **once), part, part, full, full,
                  pl.BlockSpec((te, d), lambda n: (cur(n) % nj, 0)),
                  pl.BlockSpec((te, d), lambda n: (prv(n) % nj, 0)),
                  pl.BlockSpec((tm, d), lambda n: (prv(n) // nj, 0), **once),
                  pl.BlockSpec((1, 1, d), lambda n: (boff + prv(n) // nj // bpb, 0, 5)),
                  _const_spec((1, d)), _const_spec((1, d))],
        out_specs=pl.BlockSpec((tm, d), lambda n: (prv(n) // nj, 0)),
        out_shape=jax.ShapeDtypeStruct((t, d), F32),
        scratch_shapes=[pltpu.VMEM((tm, d), F32), pltpu.VMEM((tm, te), BF16),
                        pltpu.VMEM((te, tm), F32), pltpu.VMEM((te, tm), F32)],
        compiler_params=_params(("arbitrary",)),
        name="peer_experts",
    )(h2, a1, lrow, a2, r2, u, v, x1, mod3, ln_g.reshape(1, d), ln_b.reshape(1, d))


def _trunk(x, mod3, boff, w):
    nbatch, seq, d = x.shape
    x2d = x.reshape(nbatch * seq, d)
    qkv = _proj(x2d, mod3, boff, seq, w["w_qkv"], 512, 1536, BF16, False)
    gates = _proj(x2d, mod3, boff, seq, w["w_gate"], 512, 1024, F32, True)
    ya = _window(qkv, w["sink"], nbatch, seq)
    t = nbatch * seq
    vt = qkv[:, SZ_QA + 2 * SZ_QB:SZ_QA + 2 * SZ_QB + SZ_VB].T.reshape(B_HEADS, 128, t)
    vt_aug = jnp.concatenate([vt, jnp.ones((B_HEADS, DIFF_AUG, t), BF16)], axis=1)
    yb = _diff(qkv, vt_aug, w["lam_q1"], w["lam_k1"], w["lam_q2"], w["lam_k2"], w["norm_g"], nbatch, seq, 256, 1024)
    x1, h2 = _merge(ya, yb, gates, x2d, mod3, boff, seq, w["w_up_a"], w["w_up_b"], w["w_o"],
                    w["ln1_g"], w["ln1_b"], 256)
    sel = _peer_q(h2, w["peer_wq"], w["peer_sub"], 512)
    y = _peer(h2, sel, w["peer_u"], w["peer_v"], x1, mod3, boff, seq, w["ln2_g"], w["ln2_b"], 512, 1024)
    return y.reshape(nbatch, seq, d)


def kernel(x_prompt, x_sample, c_prompt, c_sample, w_ada, b_ada, w_in, sink_a, lam_q1, lam_k1, lam_q2, lam_k2, diff_norm_g, w_up_a, w_up_b, w_o, ln1_g, ln1_b, peer_wq, peer_subkeys, peer_u, peer_v, ln2_g, ln2_b):
    layer = 0
    nbp, nbs = x_prompt.shape[0], x_sample.shape[0]
    pad = (-(nbp + nbs)) % 8
    c_all = jnp.concatenate([c_prompt, c_sample, jnp.zeros((pad, D_MODEL), F32)], axis=0)
    mod = _ada(c_all, w_ada[layer], b_ada[layer])
    mod3 = mod.reshape(mod.shape[0], 1, 6 * D_MODEL)

    wi = w_in[layer]
    o_ka = SZ_QA
    o_va = o_ka + SZ_KA
    o_qb = o_va + SZ_KA
    o_kb = o_qb + SZ_QB
    o_vb = o_kb + SZ_QB
    o_g = o_vb + SZ_VB
    w_qkv = jnp.concatenate([wi[:, :o_ka], wi[:, o_qb:o_g], wi[:, o_ka:o_qb]], axis=1).astype(BF16)
    weights = dict(
        w_qkv=w_qkv, w_gate=wi[:, o_g:].astype(BF16), sink=sink_a[layer].astype(F32),
        lam_q1=lam_q1[layer], lam_k1=lam_k1[layer], lam_q2=lam_q2[layer], lam_k2=lam_k2[layer],
        norm_g=diff_norm_g[layer],
        w_up_a=w_up_a[layer].astype(BF16), w_up_b=w_up_b[layer].astype(BF16), w_o=w_o[layer].astype(BF16),
        ln1_g=ln1_g[layer], ln1_b=ln1_b[layer],
        peer_wq=peer_wq[layer].astype(BF16), peer_sub=peer_subkeys[layer].astype(BF16),
        peer_u=peer_u[layer].astype(BF16), peer_v=peer_v[layer].astype(BF16),
        ln2_g=ln2_g[layer], ln2_b=ln2_b[layer],
    )
    y_prompt = _trunk(x_prompt, mod3, 0, weights)
    y_sample = _trunk(x_sample, mod3, nbp, weights)
    return (y_prompt, y_sample)
```

```python
import functools
import math

import jax
import jax.numpy as jnp
from jax import lax
from jax.experimental import pallas as pl
from jax.experimental.pallas import tpu as pltpu

F32 = jnp.float32
BF16 = jnp.bfloat16

D_MODEL = 2048
HEAD_DIM = 128
A_Q_HEADS = 8
A_KV_HEADS = 2
A_GROUP = A_Q_HEADS // A_KV_HEADS
WINDOW = 128
B_HEADS = 8
B_QK_DIM = 64
PEER_HEADS = 8
N_KEYS = 128
N_EXPERTS = N_KEYS * N_KEYS
PEER_TOPK = 16
SZ_QA = A_Q_HEADS * HEAD_DIM
SZ_KA = A_KV_HEADS * HEAD_DIM
SZ_QB = B_HEADS * 2 * B_QK_DIM
SZ_VB = B_HEADS * 2 * B_QK_DIM
DEPTH = 1
DN_ALPHA = (2.0 * DEPTH) ** 0.25
LN_EPS = 1e-5
RMS_EPS = 1e-5
LAM_INIT = 0.8 - 0.6 * math.exp(-0.3 * 0)

VMEM_LIMIT_V7X = 56 * 1024 * 1024

NT_DIMS = (((1,), (1,)), ((), ()))
TN_DIMS = (((0,), (0,)), ((), ()))


LANES = 128
GATE_SUB = 32

PROJ_TM, PROJ_TN_QKV, PROJ_TN_GATE = 512, 1536, 2048
DIFF_TQ, DIFF_TK = 256, 1024
MERGE_TM = 256
PEER_SELECT_TM = 512
PEER_TM, PEER_TE = 512, 1024


def _params(sem):
    return pltpu.CompilerParams(dimension_semantics=sem, vmem_limit_bytes=VMEM_LIMIT_V7X)


def _const_spec(shape):
    nd = len(shape)
    return pl.BlockSpec(shape, lambda *_: (0,) * nd, pipeline_mode=pl.Buffered(1))


def _layer_norm(r, g, b):
    mu = jnp.mean(r, axis=-1, keepdims=True)
    c = r - mu
    var = jnp.mean(c * c, axis=-1, keepdims=True)
    return c * lax.rsqrt(var + LN_EPS) * g + b


def _ada_kernel(c_ref, w_ref, b_ref, o_ref):
    c = c_ref[...]
    a = (c * jax.nn.sigmoid(c)).astype(BF16)
    o_ref[...] = jnp.dot(a, w_ref[...].astype(BF16), preferred_element_type=F32) + b_ref[...]


def _ada(c_all, w_ada, b_ada):
    nb, d = c_all.shape
    n = w_ada.shape[1]
    tn = 1024
    return pl.pallas_call(
        _ada_kernel,
        grid=(n // tn,),
        in_specs=[pl.BlockSpec((nb, d), lambda j: (0, 0)),
                  pl.BlockSpec((d, tn), lambda j: (0, j)),
                  pl.BlockSpec((1, tn), lambda j: (0, j))],
        out_specs=pl.BlockSpec((nb, tn), lambda j: (0, j)),
        out_shape=jax.ShapeDtypeStruct((nb, n), F32),
        compiler_params=_params(("arbitrary",)),
        name="ada",
    )(c_all, w_ada, b_ada.reshape(1, n))


def _proj_kernel(x_ref, sc_ref, sh_ref, w_ref, o_ref, h_scr, *, gate):
    @pl.when(pl.program_id(1) == 0)
    def _():
        h = x_ref[...] * (1.0 + sc_ref[0]) + sh_ref[0]
        h_scr[...] = h.astype(BF16)

    acc = jnp.dot(h_scr[...], w_ref[...], preferred_element_type=F32)
    if gate:
        acc = jax.nn.sigmoid(acc)
    o_ref[...] = acc.astype(o_ref.dtype)


def _proj(x2d, mod3, boff, seq, w, tm, tn, out_dtype, gate):
    t, d = x2d.shape
    n = w.shape[1]
    bpb = seq // tm
    return pl.pallas_call(
        functools.partial(_proj_kernel, gate=gate),
        grid=(t // tm, n // tn),
        in_specs=[pl.BlockSpec((tm, d), lambda i, j: (i, 0)),
                  pl.BlockSpec((1, 1, d), lambda i, j: (boff + i // bpb, 0, 1)),
                  pl.BlockSpec((1, 1, d), lambda i, j: (boff + i // bpb, 0, 0)),
                  pl.BlockSpec((d, tn), lambda i, j: (0, j))],
        out_specs=pl.BlockSpec((tm, tn), lambda i, j: (i, j)),
        out_shape=jax.ShapeDtypeStruct((t, n), out_dtype),
        scratch_shapes=[pltpu.VMEM((tm, d), BF16)],
        compiler_params=_params(("parallel", "arbitrary")),
        name="proj_gate" if gate else "proj_qkv",
    )(x2d, mod3, mod3, w)


def _window_kernel(sink_ref, q_ref, kp_ref, kc_ref, kn_ref, vp_ref, vc_ref, vn_ref, o_ref, *, seq):
    blk = WINDOW
    n = pl.program_id(1)
    krel = lax.broadcasted_iota(jnp.int32, (3 * blk, blk), 0) - blk
    qrel = lax.broadcasted_iota(jnp.int32, (3 * blk, blk), 1)
    dist_i = jnp.abs(krel - qrel)
    kabs = n * blk + krel
    valid = (dist_i <= WINDOW) & (kabs >= 0) & (kabs < seq)
    dist = dist_i.astype(F32)
    scale = HEAD_DIM ** -0.5
    for g in range(A_KV_HEADS):
        cs = slice(g * HEAD_DIM, (g + 1) * HEAD_DIM)
        k = jnp.concatenate([kp_ref[:, cs], kc_ref[:, cs], kn_ref[:, cs]], axis=0)
        v = jnp.concatenate([vp_ref[:, cs], vc_ref[:, cs], vn_ref[:, cs]], axis=0)
        for j in range(A_GROUP):
            h = g * A_GROUP + j
            slope = 2.0 ** (-8.0 * (h + 1) / A_Q_HEADS)
            hs = slice(h * HEAD_DIM, (h + 1) * HEAD_DIM)
            q = q_ref[:, hs]
            s = lax.dot_general(k, q, NT_DIMS, preferred_element_type=F32) * scale
            s = jnp.where(valid, s - slope * dist, -jnp.inf)
            sink = sink_ref[h]
            m = jnp.maximum(jnp.max(s, axis=0, keepdims=True), sink)
            p = jnp.exp(s - m)
            l = jnp.sum(p, axis=0, keepdims=True) + jnp.exp(sink - m)
            o_t = lax.dot_general(v, p.astype(BF16), TN_DIMS, preferred_element_type=F32)
            o_t = o_t * (1.0 / l)
            o_ref[:, hs] = o_t.T.astype(o_ref.dtype)


def _window(qkv, sink, nbatch, seq):
    blk = WINDOW
    nb = seq // blk
    t = nbatch * seq
    kcol = (SZ_QA + SZ_QB + 2 * SZ_VB) // (A_KV_HEADS * HEAD_DIM)
    vcol = kcol + 1

    def row(b, n, off):
        return b * nb + jnp.clip(n + off, 0, nb - 1)

    kv_specs = [pl.BlockSpec((blk, SZ_KA), functools.partial(lambda b, n, col, off: (row(b, n, off), col), col=col, off=off))
                for col in (kcol, vcol) for off in (-1, 0, 1)]
    return pl.pallas_call(
        functools.partial(_window_kernel, seq=seq),
        grid=(nbatch, nb),
        in_specs=[pl.BlockSpec(memory_space=pltpu.SMEM),
                  pl.BlockSpec((blk, SZ_QA), lambda b, n: (b * nb + n, 0))] + kv_specs,
        out_specs=pl.BlockSpec((blk, SZ_QA), lambda b, n: (b * nb + n, 0)),
        out_shape=jax.ShapeDtypeStruct((t, SZ_QA), BF16),
        compiler_params=_params(("parallel", "parallel")),
        name="window_attn",
    )(sink, qkv, qkv, qkv, qkv, qkv, qkv, qkv)


DIFF_AUG = 16
DIFF_SPLIT = 256
DIFF_QK_AHEAD = 1
DIFF_PV_BEHIND = 2
DIFF_ROWS = 128
DIFF_UNDERFLOW = 160.0
DIFF_NORM_SLACK = 1.05
DIFF_FIRST_HEADS = (0, 2, 3, 4, 5)
LOG2E = math.log2(math.e)
LOG2E_PARTS = (1.4453125, -0.00262451171875, 7.063150405883789e-06)


def _diff_step(q_bias_rows, group_shift, k_ref, kc_ref, vt_ref, qt_scr, m_scr, acc_scr, s_scr, p_scr, dist, *,
               tq, tk, on_diag, first_head=0):
    nhm = 2 * B_HEADS
    ns, npb = DIFF_QK_AHEAD + 1, DIFF_PV_BEHIND + 1
    ngrp = tk // DIFF_SPLIT
    tiles = DIFF_SPLIT // DIFF_ROWS

    def qk(hm):
        h = hm // 2
        k_aug = jnp.concatenate([k_ref[:, h * 128:(h + 1) * 128], kc_ref[...]], axis=1)
        s_scr[hm % ns] = jnp.dot(k_aug, qt_scr[hm], preferred_element_type=F32)

    def softmax(hm):
        h, slot, pslot = hm // 2, hm % ns, hm % npb
        slope = 2.0 ** (-8.0 * (h + 1) / B_HEADS)
        shifts = [group_shift(h, g) for g in range(ngrp)]
        alphas = []
        for c in range(tq // LANES):
            cs = slice(c * LANES, (c + 1) * LANES)
            m_blk = None
            for g in range(ngrp):
                mrun = None
                for rb in range(tiles):
                    rs = slice(g * DIFF_SPLIT + rb * DIFF_ROWS, g * DIFF_SPLIT + (rb + 1) * DIFF_ROWS)
                    s = s_scr[slot, rs, cs]
                    if on_diag:
                        s = s - (slope * LOG2E) * dist[rs, cs]
                        s_scr[slot, rs, cs] = s
                    mrun = s if mrun is None else jnp.maximum(mrun, s)
                mg = jnp.max(mrun, axis=0, keepdims=True)
                if shifts[g] is not None:
                    mg = mg + shifts[g]
                m_blk = mg if m_blk is None else jnp.maximum(m_blk, mg)
            m_old = m_scr[hm:hm + 1, cs]
            m_new = jnp.maximum(m_old, m_blk)
            alphas.append(jnp.exp2(m_old - m_new))
            for g in range(ngrp):
                m_g = m_new if shifts[g] is None else m_new - shifts[g]
                for rb in range(tiles):
                    rs = slice(g * DIFF_SPLIT + rb * DIFF_ROWS, g * DIFF_SPLIT + (rb + 1) * DIFF_ROWS)
                    p_scr[pslot, rs, cs] = jnp.exp2(s_scr[slot, rs, cs] - m_g).astype(BF16)
            m_scr[hm:hm + 1, cs] = m_new
        return jnp.concatenate(alphas, axis=1)

    def pv(hm, alpha):
        r = jnp.dot(vt_ref[hm // 2], p_scr[hm % npb], preferred_element_type=F32)
        acc_scr[hm] = alpha * acc_scr[hm] + r

    lo = 2 * first_head
    for hm in range(lo, nhm):
        qt_scr[hm, 128:128 + DIFF_AUG, :] = q_bias_rows(hm // 2)
    for hm in range(lo, lo + DIFF_QK_AHEAD):
        qk(hm)
    alpha = {}
    for t in range(lo, nhm + DIFF_PV_BEHIND):
        if t + DIFF_QK_AHEAD < nhm:
            qk(t + DIFF_QK_AHEAD)
        if t < nhm:
            alpha[t] = softmax(t)
        if t - DIFF_PV_BEHIND >= lo:
            pv(t - DIFF_PV_BEHIND, alpha.pop(t - DIFF_PV_BEHIND))


def _diff_kernel(qn_ref, kn_ref, lq1_ref, lk1_ref, lq2_ref, lk2_ref, ng_ref, q_ref, k_ref, kc_ref, vt_ref, o_ref,
                 qt_scr, m_scr, acc_scr, s_scr, p_scr, dist_scr, *, tq, tk):
    b = pl.program_id(0)
    qi = pl.program_id(1)
    kv = pl.program_id(2)
    nkv = pl.num_programs(2)
    nq = pl.num_programs(1)
    kd = (qi * tq) // tk
    kb = (kd + kv) % nkv

    @pl.when(kv == 0)
    def _():
        m_scr[...] = jnp.full(m_scr.shape, -jnp.inf, F32)
        acc_scr[...] = jnp.zeros(acc_scr.shape, F32)
        row = lax.broadcasted_iota(jnp.int32, (2 * B_QK_DIM, tq), 0)
        zeros = jnp.zeros((128, tq), BF16)
        for h in range(B_HEADS):
            q_t = (q_ref[:, h * 128:(h + 1) * 128].astype(F32) * (B_QK_DIM ** -0.5 * LOG2E)).T
            qt_scr[2 * h, 0:128, :] = jnp.where(row < B_QK_DIM, q_t, 0.0).astype(BF16)
            qt_scr[2 * h + 1, 0:128, :] = jnp.where(row >= B_QK_DIM, q_t, 0.0).astype(BF16)
            qt_scr[2 * h, 128:256, :] = zeros
            qt_scr[2 * h + 1, 128:256, :] = zeros

    q0 = qi * tq
    k0 = kb * tk
    left = k0 + tk <= q0
    right = k0 >= q0 + tq
    args = (k_ref, kc_ref, vt_ref, qt_scr, m_scr, acc_scr, s_scr, p_scr)

    @pl.when(jnp.logical_or(left, right))
    def _():
        sgn = jnp.where(left, 1.0, -1.0).astype(F32)
        off = (q0 - k0).astype(F32)
        rid = lax.broadcasted_iota(jnp.int32, (DIFF_AUG, tq), 0)
        il = (lax.broadcasted_iota(jnp.int32, (DIFF_AUG, tq), 1) - DIFF_SPLIT // 2).astype(F32)

        def rows(h):
            g = sgn * 2.0 ** (-8.0 * (h + 1) / B_HEADS)
            r = jnp.where(rid < 3, -g * il, 0.0)
            for part, c in enumerate(LOG2E_PARTS):
                r = jnp.where(rid == 3 + part, g * c, r)
            return r.astype(BF16)

        def shift(h, grp):
            return -(sgn * 2.0 ** (-8.0 * (h + 1) / B_HEADS) * LOG2E) * (off - float(DIFF_SPLIT * grp))

        dmin = jnp.where(left, q0 - (k0 + tk - 1), k0 - (q0 + tq - 1)).astype(F32)
        nhm = 2 * B_HEADS
        qoff = (b * nq + qi) * nhm
        koff = (b * nkv + kb) * nhm
        doff = (b * nkv + kd) * nhm
        scale = DIFF_NORM_SLACK * (B_QK_DIM ** -0.5) * LOG2E
        lead = jnp.int32(0)
        ok = jnp.bool_(True)
        for h in range(B_HEADS):
            bound = jnp.float32(0.0)
            for mp in range(2):
                hm = 2 * h + mp
                bound = jnp.maximum(bound, qn_ref[qoff + hm] * (kn_ref[koff + hm] + kn_ref[doff + hm]))
            slope = 2.0 ** (-8.0 * (h + 1) / B_HEADS)
            ok = jnp.logical_and(ok, scale * bound - slope * LOG2E * dmin < -DIFF_UNDERFLOW)
            lead = lead + ok.astype(jnp.int32)
        for idx, first in enumerate(DIFF_FIRST_HEADS):
            upper = DIFF_FIRST_HEADS[idx + 1] if idx + 1 < len(DIFF_FIRST_HEADS) else B_HEADS + 1

            @pl.when(jnp.logical_and(lead >= first, lead < upper))
            def _(first=first):
                _diff_step(rows, shift, *args, None, tq=tq, tk=tk, on_diag=False, first_head=first)

    @pl.when(jnp.logical_not(jnp.logical_or(left, right)))
    def _():
        kpos = k0 + lax.broadcasted_iota(jnp.int32, (tk, tq), 0)
        qpos = q0 + lax.broadcasted_iota(jnp.int32, (tk, tq), 1)
        dist_scr[...] = jnp.abs(kpos - qpos).astype(F32)
        _diff_step(lambda h: jnp.zeros((DIFF_AUG, tq), BF16), lambda h, g: None, *args, dist_scr,
                   tq=tq, tk=tk, on_diag=True)

    @pl.when(kv == nkv - 1)
    def _():
        lam = (jnp.exp(jnp.sum(lq1_ref[...] * lk1_ref[...], axis=-1, keepdims=True))
               - jnp.exp(jnp.sum(lq2_ref[...] * lk2_ref[...], axis=-1, keepdims=True)) + LAM_INIT)
        for h in range(B_HEADS):
            a0, a1 = acc_scr[2 * h], acc_scr[2 * h + 1]
            o0 = a0[0:128] * (1.0 / a0[128:129])
            o1 = a1[0:128] * (1.0 / a1[128:129])
            o = (o0 - lam * o1).T
            o = o * lax.rsqrt(jnp.mean(o * o, axis=-1, keepdims=True) + RMS_EPS) * ng_ref[...]
            o_ref[:, h * 128:(h + 1) * 128] = (o * (1.0 - LAM_INIT)).astype(o_ref.dtype)


def _rownorm_kernel(x_ref, ind_ref, o_ref):
    x = x_ref[...].astype(F32)
    ss = jnp.dot(x * x, ind_ref[...], preferred_element_type=F32)
    o_ref[0] = jnp.broadcast_to(jnp.sqrt(jnp.max(ss, axis=0, keepdims=True)), o_ref.shape[1:])


def _rownorm_max(qkv, col, rows):
    t = qkv.shape[0]
    ind = (jnp.arange(SZ_QB)[:, None] // B_QK_DIM == jnp.arange(LANES)[None, :]).astype(F32)
    out = pl.pallas_call(
        _rownorm_kernel,
        grid=(t // rows,),
        in_specs=[pl.BlockSpec((rows, SZ_QB), lambda i: (i, col)), _const_spec(ind.shape)],
        out_specs=pl.BlockSpec((1, 8, LANES), lambda i: (i, 0, 0)),
        out_shape=jax.ShapeDtypeStruct((t // rows, 8, LANES), F32),
        compiler_params=_params(("parallel",)),
        name="diff_rownorm",
    )(qkv, ind)
    return out[:, 0, :2 * B_HEADS].reshape(-1)


def _diff(qkv, vt_aug, lam_q1, lam_k1, lam_q2, lam_k2, norm_g, nbatch, seq, tq, tk):
    t = nbatch * seq
    nq, nk = seq // tq, seq // tk
    qn = _rownorm_max(qkv, 1, tq)
    kn = _rownorm_max(qkv, 2, tk)
    kblk = lambda i, j: ((i * tq) // tk + j) % nk
    w = SZ_QB
    nhm = 2 * B_HEADS
    small = lambda a: a.reshape(1, -1).astype(F32)
    sspec = lambda n: pl.BlockSpec((1, n), lambda b, i, j: (0, 0))
    assert tq == DIFF_SPLIT and tk % DIFF_SPLIT == 0
    jl = (jnp.arange(tk) % DIFF_SPLIT - DIFF_SPLIT // 2).astype(F32)
    kc = jnp.zeros((tk, 128), F32).at[:, 0:3].set(jnp.asarray(LOG2E_PARTS, F32)).at[:, 3:6].set(jl[:, None])
    return pl.pallas_call(
        functools.partial(_diff_kernel, tq=tq, tk=tk),
        grid=(nbatch, nq, nk),
        in_specs=[pl.BlockSpec(memory_space=pltpu.SMEM)] * 2 + [sspec(B_QK_DIM)] * 4 + [sspec(2 * B_QK_DIM),
                  pl.BlockSpec((tq, w), lambda b, i, j: (b * nq + i, 1)),
                  pl.BlockSpec((tk, w), lambda b, i, j: (b * nk + kblk(i, j), 2)),
                  _const_spec((tk, 128)),
                  pl.BlockSpec((B_HEADS, 128 + DIFF_AUG, tk), lambda b, i, j: (0, 0, b * nk + kblk(i, j)))],
        out_specs=pl.BlockSpec((tq, w), lambda b, i, j: (b * nq + i, 0)),
        out_shape=jax.ShapeDtypeStruct((t, w), BF16),
        scratch_shapes=[pltpu.VMEM((nhm, 256, tq), BF16),
                        pltpu.VMEM((nhm, tq), F32),
                        pltpu.VMEM((nhm, 128 + DIFF_AUG, tq), F32),
                        pltpu.VMEM((DIFF_QK_AHEAD + 1, tk, tq), F32),
                        pltpu.VMEM((DIFF_PV_BEHIND + 1, tk, tq), BF16),
                        pltpu.VMEM((tk, tq), F32)],
        compiler_params=_params(("parallel", "parallel", "arbitrary")),
        name="diff_attn",
    )(qn, kn, small(lam_q1), small(lam_k1), small(lam_q2), small(lam_k2), small(norm_g), qkv, qkv, kc.astype(BF16), vt_aug)


def _merge_kernel(ya_ref, yb_ref, ga_ref, gb_ref, x_ref, g1_ref, sc2_ref, sh2_ref,
                  wua_ref, wub_ref, wo_ref, lg_ref, lb_ref, x1_ref, h2_ref):
    ua = jnp.dot(ya_ref[...], wua_ref[...], preferred_element_type=F32)
    ub = jnp.dot(yb_ref[...], wub_ref[...], preferred_element_type=F32)
    merged = ga_ref[...] * ua + gb_ref[...] * ub
    z = jnp.dot(merged.astype(BF16), wo_ref[...], preferred_element_type=F32)
    r = DN_ALPHA * x_ref[...] + g1_ref[0] * z
    x1 = _layer_norm(r, lg_ref[...], lb_ref[...])
    x1_ref[...] = x1
    h2_ref[...] = (x1 * (1.0 + sc2_ref[0]) + sh2_ref[0]).astype(BF16)


def _merge(ya, yb, gates, x2d, mod3, boff, seq, wua, wub, wo, ln_g, ln_b, tm):
    t, d = x2d.shape
    bpb = seq // tm
    mspec = lambda c: pl.BlockSpec((1, 1, d), lambda i: (boff + i // bpb, 0, c))
    row = lambda n, c=0: pl.BlockSpec((tm, n), lambda i: (i, c))
    return pl.pallas_call(
        _merge_kernel,
        grid=(t // tm,),
        in_specs=[row(SZ_QA), row(SZ_VB), row(d, 0), row(d, 1), row(d),
                  mspec(2), mspec(4), mspec(3),
                  _const_spec(wua.shape), _const_spec(wub.shape), _const_spec(wo.shape),
                  _const_spec((1, d)), _const_spec((1, d))],
        out_specs=[row(d), row(d)],
        out_shape=[jax.ShapeDtypeStruct((t, d), F32), jax.ShapeDtypeStruct((t, d), BF16)],
        compiler_params=_params(("parallel",)),
        name="merge_ln1",
    )(ya, yb, gates, gates, x2d, mod3, mod3, mod3, wua, wub, wo, ln_g.reshape(1, d), ln_b.reshape(1, d))


def _top16(s, ties):
    n, t = s.shape
    iota = lax.broadcasted_iota(jnp.int32, (n, t), 0)
    rank = jnp.full((n, t), float(PEER_TOPK), F32)
    work = s
    vals = []
    for a in range(PEER_TOPK):
        m = jnp.max(work, axis=0, keepdims=True)
        sel = work == m
        if ties:
            idx = jnp.min(jnp.where(sel, iota, n), axis=0, keepdims=True)
            sel = iota == idx
        rank = jnp.where(sel, float(a), rank)
        work = jnp.where(sel, -jnp.inf, work)
        vals.append(m)
    bad = None
    if not ties:
        removed = jnp.sum(jnp.where(rank < float(PEER_TOPK), 1.0, 0.0), axis=0, keepdims=True)
        bad = jnp.where(removed != float(PEER_TOPK), 1.0, 0.0)
    return jnp.concatenate(vals, axis=0), rank, bad


def _peer_select(s1, s2, ties):
    v1, r1, bad1 = _top16(s1, ties)
    v2, r2, bad2 = _top16(s2, ties)
    half = PEER_TOPK // 2
    cand = jnp.concatenate([v1[0:1] + v2] + [v1[a:a + 1] + v2[0:half] for a in range(1, half)]
                           + [v1[half:] + v2[0:1]], axis=0)
    cv, crank, bad3 = _top16(cand, ties)
    chosen = jnp.where(crank < float(PEER_TOPK), 1.0, 0.0)
    z = jnp.sum(jnp.exp(cv - cv[0:1, :]), axis=0, keepdims=True)
    inv_z = 0.5 / z
    tail = PEER_TOPK + (half - 1) * half
    cnts = [jnp.sum(chosen[0:PEER_TOPK], axis=0, keepdims=True)]
    cnts += [jnp.sum(chosen[PEER_TOPK + (a - 1) * half:PEER_TOPK + a * half], axis=0, keepdims=True)
             for a in range(1, half)]
    cnts += [chosen[tail + a:tail + a + 1] for a in range(half)]
    lrow = jnp.zeros_like(s1)
    for a in range(PEER_TOPK):
        lrow = jnp.where(r1 == float(a), cnts[a], lrow)
    a1 = jnp.where(r1 < float(PEER_TOPK), jnp.exp(s1 - v1[0:1, :]) * inv_z, 0.0)
    a2 = jnp.exp(s2 - v2[0:1, :])
    n_bad = None if ties else jnp.sum(bad1 + bad2 + bad3)
    return (a1, lrow, a2, r2), n_bad


def _peer_q_kernel(h2_ref, wq_ref, sub_ref, a1_ref, lrow_ref, a2_ref, r2_ref, q_scr):
    hd = pl.program_id(1)
    kd = N_KEYS
    out_refs = (a1_ref, lrow_ref, a2_ref, r2_ref)

    @pl.when(hd == 0)
    def _():
        q = jnp.dot(h2_ref[...], wq_ref[...], preferred_element_type=F32).astype(BF16)
        for h in range(PEER_HEADS):
            q_scr[h] = q[:, h * 2 * kd:(h + 1) * 2 * kd]

    q = q_scr[hd]
    s1 = lax.dot_general(sub_ref[0, 0], q[:, :kd], NT_DIMS, preferred_element_type=F32)
    s2 = lax.dot_general(sub_ref[0, 1], q[:, kd:], NT_DIMS, preferred_element_type=F32)
    outs, n_bad = _peer_select(s1, s2, ties=False)
    for ref, val in zip(out_refs, outs):
        ref[0] = val

    @pl.when(n_bad > 0.0)
    def _():
        outs, _ = _peer_select(s1, s2, ties=True)
        for ref, val in zip(out_refs, outs):
            ref[0] = val


def _peer_q(h2, wq, sub, tm):
    t, d = h2.shape
    out = jax.ShapeDtypeStruct((PEER_HEADS, N_KEYS, t), F32)
    ospec = pl.BlockSpec((1, N_KEYS, tm), lambda i, h: (h, 0, i))
    return pl.pallas_call(
        _peer_q_kernel,
        grid=(t // tm, PEER_HEADS),
        in_specs=[pl.BlockSpec((tm, d), lambda i, h: (i, 0)),
                  _const_spec(wq.shape),
                  pl.BlockSpec((1, 2, N_KEYS, N_KEYS), lambda i, h: (h, 0, 0, 0))],
        out_specs=[ospec] * 4,
        out_shape=[out] * 4,
        scratch_shapes=[pltpu.VMEM((PEER_HEADS, tm, 2 * N_KEYS), BF16)],
        compiler_params=_params(("parallel", "arbitrary")),
        name="peer_select",
    )(h2, wq, sub)


def _peer_kernel(h2_ref, a1_ref, lrow_ref, a2_ref, r2_ref, u_ref, v_ref, x1_ref, g2_ref, lg_ref, lb_ref,
                 o_ref, acc_scr, w_scr, xu_scr, g_scr, *, rows, nj):
    n = pl.program_id(0)
    jp = jnp.maximum(n - 1, 0) % nj

    @pl.when(n == 0)
    def _():
        w_scr[...] = jnp.zeros(w_scr.shape, BF16)

    @pl.when(jp == 0)
    def _():
        acc_scr[...] = jnp.zeros(acc_scr.shape, F32)

    acc_scr[...] += jnp.dot(w_scr[...], v_ref[...], preferred_element_type=F32)

    tm = h2_ref.shape[0]
    for tc in range(tm // LANES):
        cs = slice(tc * LANES, (tc + 1) * LANES)
        for sb in range(N_KEYS // GATE_SUB):
            ss = slice(sb * GATE_SUB, (sb + 1) * GATE_SUB)
            g = [None] * rows
            for h in range(PEER_HEADS):
                r2 = r2_ref[h, ss, cs]
                a2 = a2_ref[h, ss, cs]
                for r in range(rows):
                    term = jnp.where(r2 < lrow_ref[h, r:r + 1, cs], a1_ref[h, r:r + 1, cs] * a2, 0.0)
                    g[r] = term if g[r] is None else g[r] + term
            for r in range(rows):
                g_scr[r * N_KEYS + sb * GATE_SUB:r * N_KEYS + (sb + 1) * GATE_SUB, cs] = g[r]

    xu_scr[...] = lax.dot_general(u_ref[...], h2_ref[...], NT_DIMS, preferred_element_type=F32)
    for r in range(rows):
        rs = slice(r * N_KEYS, (r + 1) * N_KEYS)
        for tc in range(tm // LANES):
            cs = slice(tc * LANES, (tc + 1) * LANES)
            xu = xu_scr[rs, cs]
            w_t = xu * (1.0 + lax.erf(xu * (2.0 ** -0.5))) * g_scr[rs, cs]
            w_scr[cs, rs] = w_t.T.astype(BF16)

    @pl.when(jnp.logical_and(n > 0, jp == nj - 1))
    def _():
        r = DN_ALPHA * x1_ref[...] + g2_ref[0] * acc_scr[...]
        o_ref[...] = _layer_norm(r, lg_ref[...], lb_ref[...])


def _peer(h2, sel, u, v, x1, mod3, boff, seq, ln_g, ln_b, tm, te):
    t, d = h2.shape
    rows = te // N_KEYS
    bpb = seq // tm
    nj = N_EXPERTS // te
    a1, lrow, a2, r2 = sel
    ni = t // tm
    last = ni * nj - 1
    once = dict(pipeline_mode=pl.Buffered(1))
    cur = lambda n: jnp.minimum(n, last)
    prv = lambda n: jnp.maximum(n - 1, 0)
    full = pl.BlockSpec((PEER_HEADS, N_KEYS, tm), lambda n: (0, 0, cur(n) // nj), **once)
    part = pl.BlockSpec((PEER_HEADS, rows, tm), lambda n: (0, cur(n) % nj, cur(n) // nj))
    return pl.pallas_call(
        functools.partial(_peer_kernel, rows=rows, nj=nj),
        grid=(ni * nj + 1,),
        in_specs=[pl.BlockSpec((tm, d), lambda n: (cur(n) // nj, 0), **once), part, part, full, full,
                  pl.BlockSpec((te, d), lambda n: (cur(n) % nj, 0)),
                  pl.BlockSpec((te, d), lambda n: (prv(n) % nj, 0)),
                  pl.BlockSpec((tm, d), lambda n: (prv(n) // nj, 0), **once),
                  pl.BlockSpec((1, 1, d), lambda n: (boff + prv(n) // nj // bpb, 0, 5)),
                  _const_spec((1, d)), _const_spec((1, d))],
        out_specs=pl.BlockSpec((tm, d), lambda n: (prv(n) // nj, 0)),
        out_shape=jax.ShapeDtypeStruct((t, d), F32),
        scratch_shapes=[pltpu.VMEM((tm, d), F32), pltpu.VMEM((tm, te), BF16),
                        pltpu.VMEM((te, tm), F32), pltpu.VMEM((te, tm), F32)],
        compiler_params=_params(("arbitrary",)),
        name="peer_experts",
    )(h2, a1, lrow, a2, r2, u, v, x1, mod3, ln_g.reshape(1, d), ln_b.reshape(1, d))


def _trunk(x, mod3, boff, w):
    nbatch, seq, d = x.shape
    x2d = x.reshape(nbatch * seq, d)
    qkv = _proj(x2d, mod3, boff, seq, w["w_qkv"], PROJ_TM, PROJ_TN_QKV, BF16, False)
    gates = _proj(x2d, mod3, boff, seq, w["w_gate"], PROJ_TM, PROJ_TN_GATE, BF16, True)
    ya = _window(qkv, w["sink"], nbatch, seq)
    t = nbatch * seq
    vt = qkv[:, SZ_QA + 2 * SZ_QB:SZ_QA + 2 * SZ_QB + SZ_VB].T.reshape(B_HEADS, 128, t)
    vt_aug = jnp.concatenate([vt, jnp.ones((B_HEADS, DIFF_AUG, t), BF16)], axis=1)
    yb = _diff(qkv, vt_aug, w["lam_q1"], w["lam_k1"], w["lam_q2"], w["lam_k2"], w["norm_g"], nbatch, seq,
               DIFF_TQ, DIFF_TK)
    x1, h2 = _merge(ya, yb, gates, x2d, mod3, boff, seq, w["w_up_a"], w["w_up_b"], w["w_o"],
                    w["ln1_g"], w["ln1_b"], MERGE_TM)
    sel = _peer_q(h2, w["peer_wq"], w["peer_sub"], PEER_SELECT_TM)
    y = _peer(h2, sel, w["peer_u"], w["peer_v"], x1, mod3, boff, seq, w["ln2_g"], w["ln2_b"], PEER_TM, PEER_TE)
    return y.reshape(nbatch, seq, d)


def kernel(x_prompt, x_sample, c_prompt, c_sample, w_ada, b_ada, w_in, sink_a, lam_q1, lam_k1, lam_q2, lam_k2, diff_norm_g, w_up_a, w_up_b, w_o, ln1_g, ln1_b, peer_wq, peer_subkeys, peer_u, peer_v, ln2_g, ln2_b):
    layer = 0
    nbp, nbs = x_prompt.shape[0], x_sample.shape[0]
    pad = (-(nbp + nbs)) % 8
    c_all = jnp.concatenate([c_prompt, c_sample, jnp.zeros((pad, D_MODEL), F32)], axis=0)
    mod = _ada(c_all, w_ada[layer], b_ada[layer])
    mod3 = mod.reshape(mod.shape[0], 1, 6 * D_MODEL)

    wi = w_in[layer]
    o_ka = SZ_QA
    o_va = o_ka + SZ_KA
    o_qb = o_va + SZ_KA
    o_kb = o_qb + SZ_QB
    o_vb = o_kb + SZ_QB
    o_g = o_vb + SZ_VB
    w_qkv = jnp.concatenate([wi[:, :o_ka], wi[:, o_qb:o_g], wi[:, o_ka:o_qb]], axis=1).astype(BF16)
    weights = dict(
        w_qkv=w_qkv, w_gate=wi[:, o_g:].astype(BF16), sink=sink_a[layer].astype(F32),
        lam_q1=lam_q1[layer], lam_k1=lam_k1[layer], lam_q2=lam_q2[layer], lam_k2=lam_k2[layer],
        norm_g=diff_norm_g[layer],
        w_up_a=w_up_a[layer].astype(BF16), w_up_b=w_up_b[layer].astype(BF16), w_o=w_o[layer].astype(BF16),
        ln1_g=ln1_g[layer], ln1_b=ln1_b[layer],
        peer_wq=peer_wq[layer].astype(BF16), peer_sub=peer_subkeys[layer].astype(BF16),
        peer_u=peer_u[layer].astype(BF16), peer_v=peer_v[layer].astype(BF16),
        ln2_g=ln2_g[layer], ln2_b=ln2_b[layer],
    )
    y_prompt = _trunk(x_prompt, mod3, 0, weights)
    y_sample = _trunk(x_sample, mod3, nbp, weights)
    return (y_prompt, y_sample)
```

```python
import functools
import math

import jax
import jax.numpy as jnp
from jax import lax
from jax.experimental import pallas as pl
from jax.experimental.pallas import tpu as pltpu

F32 = jnp.float32
BF16 = jnp.bfloat16

D_MODEL = 2048
HEAD_DIM = 128
A_Q_HEADS = 8
A_KV_HEADS = 2
A_GROUP = A_Q_HEADS // A_KV_HEADS
WINDOW = 128
B_HEADS = 8
B_QK_DIM = 64
PEER_HEADS = 8
N_KEYS = 128
N_EXPERTS = N_KEYS * N_KEYS
PEER_TOPK = 16
SZ_QA = A_Q_HEADS * HEAD_DIM
SZ_KA = A_KV_HEADS * HEAD_DIM
SZ_QB = B_HEADS * 2 * B_QK_DIM
SZ_VB = B_HEADS * 2 * B_QK_DIM
DEPTH = 1
DN_ALPHA = (2.0 * DEPTH) ** 0.25
LN_EPS = 1e-5
RMS_EPS = 1e-5
LAM_INIT = 0.8 - 0.6 * math.exp(-0.3 * 0)

VMEM_LIMIT_V7X = 56 * 1024 * 1024

NT_DIMS = (((1,), (1,)), ((), ()))
TN_DIMS = (((0,), (0,)), ((), ()))


LANES = 128
GATE_SUB = 32

PROJ_TM, PROJ_TN_QKV, PROJ_TN_GATE = 512, 1536, 2048
DIFF_TQ, DIFF_TK = 256, 1024
MERGE_TM = 256
PEER_SELECT_TM = 512
PEER_TM, PEER_TE = 512, 1024


def _params(sem):
    return pltpu.CompilerParams(dimension_semantics=sem, vmem_limit_bytes=VMEM_LIMIT_V7X)


def _const_spec(shape):
    nd = len(shape)
    return pl.BlockSpec(shape, lambda *_: (0,) * nd, pipeline_mode=pl.Buffered(1))


def _layer_norm(r, g, b):
    mu = jnp.mean(r, axis=-1, keepdims=True)
    c = r - mu
    var = jnp.mean(c * c, axis=-1, keepdims=True)
    return c * lax.rsqrt(var + LN_EPS) * g + b


def _ada_kernel(c_ref, w_ref, b_ref, o_ref):
    c = c_ref[...]
    a = (c * jax.nn.sigmoid(c)).astype(BF16)
    o_ref[...] = jnp.dot(a, w_ref[...].astype(BF16), preferred_element_type=F32) + b_ref[...]


def _ada(c_all, w_ada, b_ada):
    nb, d = c_all.shape
    n = w_ada.shape[1]
    tn = 1024
    return pl.pallas_call(
        _ada_kernel,
        grid=(n // tn,),
        in_specs=[pl.BlockSpec((nb, d), lambda j: (0, 0)),
                  pl.BlockSpec((d, tn), lambda j: (0, j)),
                  pl.BlockSpec((1, tn), lambda j: (0, j))],
        out_specs=pl.BlockSpec((nb, tn), lambda j: (0, j)),
        out_shape=jax.ShapeDtypeStruct((nb, n), F32),
        compiler_params=_params(("arbitrary",)),
        name="ada",
    )(c_all, w_ada, b_ada.reshape(1, n))


def _proj_kernel(x_ref, sc_ref, sh_ref, w_ref, o_ref, h_scr, *, gate):
    @pl.when(pl.program_id(1) == 0)
    def _():
        h = x_ref[...] * (1.0 + sc_ref[0]) + sh_ref[0]
        h_scr[...] = h.astype(BF16)

    acc = jnp.dot(h_scr[...], w_ref[...], preferred_element_type=F32)
    if gate:
        acc = jax.nn.sigmoid(acc)
    o_ref[...] = acc.astype(o_ref.dtype)


def _proj(x2d, mod3, boff, seq, w, tm, tn, out_dtype, gate):
    t, d = x2d.shape
    n = w.shape[1]
    bpb = seq // tm
    return pl.pallas_call(
        functools.partial(_proj_kernel, gate=gate),
        grid=(t // tm, n // tn),
        in_specs=[pl.BlockSpec((tm, d), lambda i, j: (i, 0)),
                  pl.BlockSpec((1, 1, d), lambda i, j: (boff + i // bpb, 0, 1)),
                  pl.BlockSpec((1, 1, d), lambda i, j: (boff + i // bpb, 0, 0)),
                  pl.BlockSpec((d, tn), lambda i, j: (0, j))],
        out_specs=pl.BlockSpec((tm, tn), lambda i, j: (i, j)),
        out_shape=jax.ShapeDtypeStruct((t, n), out_dtype),
        scratch_shapes=[pltpu.VMEM((tm, d), BF16)],
        compiler_params=_params(("parallel", "arbitrary")),
        name="proj_gate" if gate else "proj_qkv",
    )(x2d, mod3, mod3, w)


def _window_kernel(sink_ref, q_ref, kp_ref, kc_ref, kn_ref, vp_ref, vc_ref, vn_ref, o_ref, *, seq):
    blk = WINDOW
    n = pl.program_id(1)
    krel = lax.broadcasted_iota(jnp.int32, (3 * blk, blk), 0) - blk
    qrel = lax.broadcasted_iota(jnp.int32, (3 * blk, blk), 1)
    dist_i = jnp.abs(krel - qrel)
    kabs = n * blk + krel
    valid = (dist_i <= WINDOW) & (kabs >= 0) & (kabs < seq)
    dist = dist_i.astype(F32)
    scale = HEAD_DIM ** -0.5
    for g in range(A_KV_HEADS):
        cs = slice(g * HEAD_DIM, (g + 1) * HEAD_DIM)
        k = jnp.concatenate([kp_ref[:, cs], kc_ref[:, cs], kn_ref[:, cs]], axis=0)
        v = jnp.concatenate([vp_ref[:, cs], vc_ref[:, cs], vn_ref[:, cs]], axis=0)
        for j in range(A_GROUP):
            h = g * A_GROUP + j
            slope = 2.0 ** (-8.0 * (h + 1) / A_Q_HEADS)
            hs = slice(h * HEAD_DIM, (h + 1) * HEAD_DIM)
            q = q_ref[:, hs]
            s = lax.dot_general(k, q, NT_DIMS, preferred_element_type=F32) * scale
            s = jnp.where(valid, s - slope * dist, -jnp.inf)
            sink = sink_ref[h]
            m = jnp.maximum(jnp.max(s, axis=0, keepdims=True), sink)
            p = jnp.exp(s - m)
            l = jnp.sum(p, axis=0, keepdims=True) + jnp.exp(sink - m)
            o_t = lax.dot_general(v, p.astype(BF16), TN_DIMS, preferred_element_type=F32)
            o_t = o_t * (1.0 / l)
            o_ref[:, hs] = o_t.T.astype(o_ref.dtype)


def _window(qkv, sink, nbatch, seq):
    blk = WINDOW
    nb = seq // blk
    t = nbatch * seq
    kcol = (SZ_QA + SZ_QB + 2 * SZ_VB) // (A_KV_HEADS * HEAD_DIM)
    vcol = kcol + 1

    def row(b, n, off):
        return b * nb + jnp.clip(n + off, 0, nb - 1)

    kv_specs = [pl.BlockSpec((blk, SZ_KA), functools.partial(lambda b, n, col, off: (row(b, n, off), col), col=col, off=off))
                for col in (kcol, vcol) for off in (-1, 0, 1)]
    return pl.pallas_call(
        functools.partial(_window_kernel, seq=seq),
        grid=(nbatch, nb),
        in_specs=[pl.BlockSpec(memory_space=pltpu.SMEM),
                  pl.BlockSpec((blk, SZ_QA), lambda b, n: (b * nb + n, 0))] + kv_specs,
        out_specs=pl.BlockSpec((blk, SZ_QA), lambda b, n: (b * nb + n, 0)),
        out_shape=jax.ShapeDtypeStruct((t, SZ_QA), BF16),
        compiler_params=_params(("parallel", "parallel")),
        name="window_attn",
    )(sink, qkv, qkv, qkv, qkv, qkv, qkv, qkv)


DIFF_AUG = 16
DIFF_SPLIT = 256
DIFF_QK_AHEAD = 1
DIFF_PV_BEHIND = 2
DIFF_ROWS = 128
DIFF_UNDERFLOW = 160.0
DIFF_NORM_SLACK = 1.05
DIFF_FIRST_HEADS = (0, 2, 3, 4, 5)
LOG2E = math.log2(math.e)
LOG2E_PARTS = (1.4453125, -0.00262451171875, 7.063150405883789e-06)


def _diff_step(q_bias_rows, group_shift, k_ref, kc_ref, vt_ref, qt_scr, m_scr, acc_scr, s_scr, p_scr, dist, *,
               tq, tk, on_diag, first_head=0):
    nhm = 2 * B_HEADS
    ns, npb = DIFF_QK_AHEAD + 1, DIFF_PV_BEHIND + 1
    ngrp = tk // DIFF_SPLIT
    tiles = DIFF_SPLIT // DIFF_ROWS

    def qk(hm):
        h = hm // 2
        k_aug = jnp.concatenate([k_ref[:, h * 128:(h + 1) * 128], kc_ref[...]], axis=1)
        s_scr[hm % ns] = jnp.dot(k_aug, qt_scr[hm], preferred_element_type=F32)

    def softmax(hm):
        h, slot, pslot = hm // 2, hm % ns, hm % npb
        slope = 2.0 ** (-8.0 * (h + 1) / B_HEADS)
        shifts = [group_shift(h, g) for g in range(ngrp)]
        alphas = []
        for c in range(tq // LANES):
            cs = slice(c * LANES, (c + 1) * LANES)
            m_blk = None
            for g in range(ngrp):
                mrun = None
                for rb in range(tiles):
                    rs = slice(g * DIFF_SPLIT + rb * DIFF_ROWS, g * DIFF_SPLIT + (rb + 1) * DIFF_ROWS)
                    s = s_scr[slot, rs, cs]
                    if on_diag:
                        s = s - (slope * LOG2E) * dist[rs, cs]
                        s_scr[slot, rs, cs] = s
                    mrun = s if mrun is None else jnp.maximum(mrun, s)
                mg = jnp.max(mrun, axis=0, keepdims=True)
                if shifts[g] is not None:
                    mg = mg + shifts[g]
                m_blk = mg if m_blk is None else jnp.maximum(m_blk, mg)
            m_old = m_scr[hm:hm + 1, cs]
            m_new = jnp.maximum(m_old, m_blk)
            alphas.append(jnp.exp2(m_old - m_new))
            for g in range(ngrp):
                m_g = m_new if shifts[g] is None else m_new - shifts[g]
                for rb in range(tiles):
                    rs = slice(g * DIFF_SPLIT + rb * DIFF_ROWS, g * DIFF_SPLIT + (rb + 1) * DIFF_ROWS)
                    p_scr[pslot, rs, cs] = jnp.exp2(s_scr[slot, rs, cs] - m_g).astype(BF16)
            m_scr[hm:hm + 1, cs] = m_new
        return jnp.concatenate(alphas, axis=1)

    def pv(hm, alpha):
        r = jnp.dot(vt_ref[hm // 2], p_scr[hm % npb], preferred_element_type=F32)
        acc_scr[hm] = alpha * acc_scr[hm] + r

    lo = 2 * first_head
    for hm in range(lo, nhm):
        qt_scr[hm, 128:128 + DIFF_AUG, :] = q_bias_rows(hm // 2)
    for hm in range(lo, lo + DIFF_QK_AHEAD):
        qk(hm)
    alpha = {}
    for t in range(lo, nhm + DIFF_PV_BEHIND):
        if t + DIFF_QK_AHEAD < nhm:
            qk(t + DIFF_QK_AHEAD)
        if t < nhm:
            alpha[t] = softmax(t)
        if t - DIFF_PV_BEHIND >= lo:
            pv(t - DIFF_PV_BEHIND, alpha.pop(t - DIFF_PV_BEHIND))


def _diff_kernel(qn_ref, kn_ref, lq1_ref, lk1_ref, lq2_ref, lk2_ref, ng_ref, q_ref, k_ref, kc_ref, vt_ref, o_ref,
                 qt_scr, m_scr, acc_scr, s_scr, p_scr, dist_scr, *, tq, tk):
    b = pl.program_id(0)
    qi = pl.program_id(1)
    kv = pl.program_id(2)
    nkv = pl.num_programs(2)
    nq = pl.num_programs(1)
    kd = (qi * tq) // tk
    kb = (kd + kv) % nkv

    @pl.when(kv == 0)
    def _():
        m_scr[...] = jnp.full(m_scr.shape, -jnp.inf, F32)
        acc_scr[...] = jnp.zeros(acc_scr.shape, F32)
        row = lax.broadcasted_iota(jnp.int32, (2 * B_QK_DIM, tq), 0)
        zeros = jnp.zeros((128, tq), BF16)
        for h in range(B_HEADS):
            q_t = (q_ref[:, h * 128:(h + 1) * 128].astype(F32) * (B_QK_DIM ** -0.5 * LOG2E)).T
            qt_scr[2 * h, 0:128, :] = jnp.where(row < B_QK_DIM, q_t, 0.0).astype(BF16)
            qt_scr[2 * h + 1, 0:128, :] = jnp.where(row >= B_QK_DIM, q_t, 0.0).astype(BF16)
            qt_scr[2 * h, 128:256, :] = zeros
            qt_scr[2 * h + 1, 128:256, :] = zeros

    q0 = qi * tq
    k0 = kb * tk
    left = k0 + tk <= q0
    right = k0 >= q0 + tq
    args = (k_ref, kc_ref, vt_ref, qt_scr, m_scr, acc_scr, s_scr, p_scr)

    @pl.when(jnp.logical_or(left, right))
    def _():
        sgn = jnp.where(left, 1.0, -1.0).astype(F32)
        off = (q0 - k0).astype(F32)
        rid = lax.broadcasted_iota(jnp.int32, (DIFF_AUG, tq), 0)
        il = (lax.broadcasted_iota(jnp.int32, (DIFF_AUG, tq), 1) - DIFF_SPLIT // 2).astype(F32)

        def rows(h):
            g = sgn * 2.0 ** (-8.0 * (h + 1) / B_HEADS)
            r = jnp.where(rid < 3, -g * il, 0.0)
            for part, c in enumerate(LOG2E_PARTS):
                r = jnp.where(rid == 3 + part, g * c, r)
            return r.astype(BF16)

        def shift(h, grp):
            return -(sgn * 2.0 ** (-8.0 * (h + 1) / B_HEADS) * LOG2E) * (off - float(DIFF_SPLIT * grp))

        dmin = jnp.where(left, q0 - (k0 + tk - 1), k0 - (q0 + tq - 1)).astype(F32)
        nhm = 2 * B_HEADS
        qoff = (b * nq + qi) * nhm
        koff = (b * nkv + kb) * nhm
        doff = (b * nkv + kd) * nhm
        scale = DIFF_NORM_SLACK * (B_QK_DIM ** -0.5) * LOG2E
        lead = jnp.int32(0)
        ok = jnp.bool_(True)
        for h in range(B_HEADS):
            bound = jnp.float32(0.0)
            for mp in range(2):
                hm = 2 * h + mp
                bound = jnp.maximum(bound, qn_ref[qoff + hm] * (kn_ref[koff + hm] + kn_ref[doff + hm]))
            slope = 2.0 ** (-8.0 * (h + 1) / B_HEADS)
            ok = jnp.logical_and(ok, scale * bound - slope * LOG2E * dmin < -DIFF_UNDERFLOW)
            lead = lead + ok.astype(jnp.int32)
        for idx, first in enumerate(DIFF_FIRST_HEADS):
            upper = DIFF_FIRST_HEADS[idx + 1] if idx + 1 < len(DIFF_FIRST_HEADS) else B_HEADS + 1

            @pl.when(jnp.logical_and(lead >= first, lead < upper))
            def _(first=first):
                _diff_step(rows, shift, *args, None, tq=tq, tk=tk, on_diag=False, first_head=first)

    @pl.when(jnp.logical_not(jnp.logical_or(left, right)))
    def _():
        kpos = k0 + lax.broadcasted_iota(jnp.int32, (tk, tq), 0)
        qpos = q0 + lax.broadcasted_iota(jnp.int32, (tk, tq), 1)
        dist_scr[...] = jnp.abs(kpos - qpos).astype(F32)
        _diff_step(lambda h: jnp.zeros((DIFF_AUG, tq), BF16), lambda h, g: None, *args, dist_scr,
                   tq=tq, tk=tk, on_diag=True)

    @pl.when(kv == nkv - 1)
    def _():
        lam = (jnp.exp(jnp.sum(lq1_ref[...] * lk1_ref[...], axis=-1, keepdims=True))
               - jnp.exp(jnp.sum(lq2_ref[...] * lk2_ref[...], axis=-1, keepdims=True)) + LAM_INIT)
        for h in range(B_HEADS):
            a0, a1 = acc_scr[2 * h], acc_scr[2 * h + 1]
            o0 = a0[0:128] * (1.0 / a0[128:129])
            o1 = a1[0:128] * (1.0 / a1[128:129])
            o = (o0 - lam * o1).T
            o = o * lax.rsqrt(jnp.mean(o * o, axis=-1, keepdims=True) + RMS_EPS) * ng_ref[...]
            o_ref[:, h * 128:(h + 1) * 128] = (o * (1.0 - LAM_INIT)).astype(o_ref.dtype)


def _rownorm_kernel(x_ref, ind_ref, o_ref):
    x = x_ref[...].astype(F32)
    ss = jnp.dot(x * x, ind_ref[...], preferred_element_type=F32)
    o_ref[0] = jnp.broadcast_to(jnp.sqrt(jnp.max(ss, axis=0, keepdims=True)), o_ref.shape[1:])


def _rownorm_max(qkv, col, rows):
    t = qkv.shape[0]
    ind = (jnp.arange(SZ_QB)[:, None] // B_QK_DIM == jnp.arange(LANES)[None, :]).astype(F32)
    out = pl.pallas_call(
        _rownorm_kernel,
        grid=(t // rows,),
        in_specs=[pl.BlockSpec((rows, SZ_QB), lambda i: (i, col)), _const_spec(ind.shape)],
        out_specs=pl.BlockSpec((1, 8, LANES), lambda i: (i, 0, 0)),
        out_shape=jax.ShapeDtypeStruct((t // rows, 8, LANES), F32),
        compiler_params=_params(("parallel",)),
        name="diff_rownorm",
    )(qkv, ind)
    return out[:, 0, :2 * B_HEADS].reshape(-1)


def _diff(qkv, vt_aug, lam_q1, lam_k1, lam_q2, lam_k2, norm_g, nbatch, seq, tq, tk):
    t = nbatch * seq
    nq, nk = seq // tq, seq // tk
    qn = _rownorm_max(qkv, 1, tq)
    kn = _rownorm_max(qkv, 2, tk)
    kblk = lambda i, j: ((i * tq) // tk + j) % nk
    w = SZ_QB
    nhm = 2 * B_HEADS
    small = lambda a: a.reshape(1, -1).astype(F32)
    sspec = lambda n: pl.BlockSpec((1, n), lambda b, i, j: (0, 0))
    assert tq == DIFF_SPLIT and tk % DIFF_SPLIT == 0
    jl = (jnp.arange(tk) % DIFF_SPLIT - DIFF_SPLIT // 2).astype(F32)
    kc = jnp.zeros((tk, 128), F32).at[:, 0:3].set(jnp.asarray(LOG2E_PARTS, F32)).at[:, 3:6].set(jl[:, None])
    return pl.pallas_call(
        functools.partial(_diff_kernel, tq=tq, tk=tk),
        grid=(nbatch, nq, nk),
        in_specs=[pl.BlockSpec(memory_space=pltpu.SMEM)] * 2 + [sspec(B_QK_DIM)] * 4 + [sspec(2 * B_QK_DIM),
                  pl.BlockSpec((tq, w), lambda b, i, j: (b * nq + i, 1)),
                  pl.BlockSpec((tk, w), lambda b, i, j: (b * nk + kblk(i, j), 2)),
                  _const_spec((tk, 128)),
                  pl.BlockSpec((B_HEADS, 128 + DIFF_AUG, tk), lambda b, i, j: (0, 0, b * nk + kblk(i, j)))],
        out_specs=pl.BlockSpec((tq, w), lambda b, i, j: (b * nq + i, 0)),
        out_shape=jax.ShapeDtypeStruct((t, w), BF16),
        scratch_shapes=[pltpu.VMEM((nhm, 256, tq), BF16),
                        pltpu.VMEM((nhm, tq), F32),
                        pltpu.VMEM((nhm, 128 + DIFF_AUG, tq), F32),
                        pltpu.VMEM((DIFF_QK_AHEAD + 1, tk, tq), F32),
                        pltpu.VMEM((DIFF_PV_BEHIND + 1, tk, tq), BF16),
                        pltpu.VMEM((tk, tq), F32)],
        compiler_params=_params(("parallel", "parallel", "arbitrary")),
        name="diff_attn",
    )(qn, kn, small(lam_q1), small(lam_k1), small(lam_q2), small(lam_k2), small(norm_g), qkv, qkv, kc.astype(BF16), vt_aug)


def _merge_kernel(ya_ref, yb_ref, ga_ref, gb_ref, x_ref, g1_ref, sc2_ref, sh2_ref,
                  wua_ref, wub_ref, wo_ref, lg_ref, lb_ref, x1_ref, h2_ref):
    ua = jnp.dot(ya_ref[...], wua_ref[...], preferred_element_type=F32)
    ub = jnp.dot(yb_ref[...], wub_ref[...], preferred_element_type=F32)
    merged = ga_ref[...] * ua + gb_ref[...] * ub
    z = jnp.dot(merged.astype(BF16), wo_ref[...], preferred_element_type=F32)
    r = DN_ALPHA * x_ref[...] + g1_ref[0] * z
    x1 = _layer_norm(r, lg_ref[...], lb_ref[...])
    x1_ref[...] = x1
    h2_ref[...] = (x1 * (1.0 + sc2_ref[0]) + sh2_ref[0]).astype(BF16)


def _merge(ya, yb, gates, x2d, mod3, boff, seq, wua, wub, wo, ln_g, ln_b, tm):
    t, d = x2d.shape
    bpb = seq // tm
    mspec = lambda c: pl.BlockSpec((1, 1, d), lambda i: (boff + i // bpb, 0, c))
    row = lambda n, c=0: pl.BlockSpec((tm, n), lambda i: (i, c))
    return pl.pallas_call(
        _merge_kernel,
        grid=(t // tm,),
        in_specs=[row(SZ_QA), row(SZ_VB), row(d, 0), row(d, 1), row(d),
                  mspec(2), mspec(4), mspec(3),
                  _const_spec(wua.shape), _const_spec(wub.shape), _const_spec(wo.shape),
                  _const_spec((1, d)), _const_spec((1, d))],
        out_specs=[row(d), row(d)],
        out_shape=[jax.ShapeDtypeStruct((t, d), F32), jax.ShapeDtypeStruct((t, d), BF16)],
        compiler_params=_params(("parallel",)),
        name="merge_ln1",
    )(ya, yb, gates, gates, x2d, mod3, mod3, mod3, wua, wub, wo, ln_g.reshape(1, d), ln_b.reshape(1, d))


def _top16(s, ties):
    n, t = s.shape
    iota = lax.broadcasted_iota(jnp.int32, (n, t), 0)
    rank = jnp.full((n, t), float(PEER_TOPK), F32)
    work = s
    vals = []
    for a in range(PEER_TOPK):
        m = jnp.max(work, axis=0, keepdims=True)
        sel = work == m
        if ties:
            idx = jnp.min(jnp.where(sel, iota, n), axis=0, keepdims=True)
            sel = iota == idx
        rank = jnp.where(sel, float(a), rank)
        work = jnp.where(sel, -jnp.inf, work)
        vals.append(m)
    bad = None
    if not ties:
        removed = jnp.sum(jnp.where(rank < float(PEER_TOPK), 1.0, 0.0), axis=0, keepdims=True)
        bad = jnp.where(removed != float(PEER_TOPK), 1.0, 0.0)
    return jnp.concatenate(vals, axis=0), rank, bad


def _peer_select(s1, s2, ties):
    v1, r1, bad1 = _top16(s1, ties)
    v2, r2, bad2 = _top16(s2, ties)
    half = PEER_TOPK // 2
    cand = jnp.concatenate([v1[0:1] + v2] + [v1[a:a + 1] + v2[0:half] for a in range(1, half)]
                           + [v1[half:] + v2[0:1]], axis=0)
    cv, crank, bad3 = _top16(cand, ties)
    chosen = jnp.where(crank < float(PEER_TOPK), 1.0, 0.0)
    z = jnp.sum(jnp.exp(cv - cv[0:1, :]), axis=0, keepdims=True)
    inv_z = 0.5 / z
    tail = PEER_TOPK + (half - 1) * half
    cnts = [jnp.sum(chosen[0:PEER_TOPK], axis=0, keepdims=True)]
    cnts += [jnp.sum(chosen[PEER_TOPK + (a - 1) * half:PEER_TOPK + a * half], axis=0, keepdims=True)
             for a in range(1, half)]
    cnts += [chosen[tail + a:tail + a + 1] for a in range(half)]
    lrow = jnp.zeros_like(s1)
    for a in range(PEER_TOPK):
        lrow = jnp.where(r1 == float(a), cnts[a], lrow)
    a1 = jnp.where(r1 < float(PEER_TOPK), jnp.exp(s1 - v1[0:1, :]) * inv_z, 0.0)
    a2 = jnp.exp(s2 - v2[0:1, :])
    bad = None if ties else bad1 + bad2 + bad3
    return (a1, lrow, a2, r2), bad


def _peer_q_kernel(h2_ref, wq_ref, sub_ref, a1_ref, lrow_ref, a2_ref, r2_ref, q_scr):
    hd = pl.program_id(1)
    kd = N_KEYS
    out_refs = (a1_ref, lrow_ref, a2_ref, r2_ref)

    @pl.when(hd == 0)
    def _():
        q = jnp.dot(h2_ref[...], wq_ref[...], preferred_element_type=F32).astype(BF16)
        for h in range(PEER_HEADS):
            q_scr[h] = q[:, h * 2 * kd:(h + 1) * 2 * kd]

    q = q_scr[hd]
    s1 = lax.dot_general(sub_ref[0, 0], q[:, :kd], NT_DIMS, preferred_element_type=F32)
    s2 = lax.dot_general(sub_ref[0, 1], q[:, kd:], NT_DIMS, preferred_element_type=F32)
    outs, bad = _peer_select(s1, s2, ties=False)
    for ref, val in zip(out_refs, outs):
        ref[0] = val

    @pl.when(jnp.sum(bad) > 0.0)
    def _():
        for c in range(s1.shape[1] // LANES):
            cs = slice(c * LANES, (c + 1) * LANES)

            @pl.when(jnp.sum(bad[:, cs]) > 0.0)
            def _(cs=cs):
                redo, _ = _peer_select(s1[:, cs], s2[:, cs], ties=True)
                for ref, val in zip(out_refs, redo):
                    ref[0, :, cs] = val


def _peer_q(h2, wq, sub, tm):
    t, d = h2.shape
    out = jax.ShapeDtypeStruct((PEER_HEADS, N_KEYS, t), F32)
    ospec = pl.BlockSpec((1, N_KEYS, tm), lambda i, h: (h, 0, i))
    return pl.pallas_call(
        _peer_q_kernel,
        grid=(t // tm, PEER_HEADS),
        in_specs=[pl.BlockSpec((tm, d), lambda i, h: (i, 0)),
                  _const_spec(wq.shape),
                  pl.BlockSpec((1, 2, N_KEYS, N_KEYS), lambda i, h: (h, 0, 0, 0))],
        out_specs=[ospec] * 4,
        out_shape=[out] * 4,
        scratch_shapes=[pltpu.VMEM((PEER_HEADS, tm, 2 * N_KEYS), BF16)],
        compiler_params=_params(("parallel", "arbitrary")),
        name="peer_select",
    )(h2, wq, sub)


def _peer_kernel(h2_ref, a1_ref, lrow_ref, a2_ref, r2_ref, u_ref, v_ref, x1_ref, g2_ref, lg_ref, lb_ref,
                 o_ref, acc_scr, w_scr, xu_scr, g_scr, *, rows, nj):
    n = pl.program_id(0)
    jp = jnp.maximum(n - 1, 0) % nj

    @pl.when(n == 0)
    def _():
        w_scr[...] = jnp.zeros(w_scr.shape, BF16)

    @pl.when(jp == 0)
    def _():
        acc_scr[...] = jnp.zeros(acc_scr.shape, F32)

    acc_scr[...] += jnp.dot(w_scr[...], v_ref[...], preferred_element_type=F32)

    tm = h2_ref.shape[0]
    for tc in range(tm // LANES):
        cs = slice(tc * LANES, (tc + 1) * LANES)
        for sb in range(N_KEYS // GATE_SUB):
            ss = slice(sb * GATE_SUB, (sb + 1) * GATE_SUB)
            g = [None] * rows
            for h in range(PEER_HEADS):
                r2 = r2_ref[h, ss, cs]
                a2 = a2_ref[h, ss, cs]
                for r in range(rows):
                    term = jnp.where(r2 < lrow_ref[h, r:r + 1, cs], a1_ref[h, r:r + 1, cs] * a2, 0.0)
                    g[r] = term if g[r] is None else g[r] + term
            for r in range(rows):
                g_scr[r * N_KEYS + sb * GATE_SUB:r * N_KEYS + (sb + 1) * GATE_SUB, cs] = g[r]

    xu_scr[...] = lax.dot_general(u_ref[...], h2_ref[...], NT_DIMS, preferred_element_type=F32)
    for r in range(rows):
        rs = slice(r * N_KEYS, (r + 1) * N_KEYS)
        for tc in range(tm // LANES):
            cs = slice(tc * LANES, (tc + 1) * LANES)
            xu = xu_scr[rs, cs]
            w_t = xu * (1.0 + lax.erf(xu * (2.0 ** -0.5))) * g_scr[rs, cs]
            w_scr[cs, rs] = w_t.T.astype(BF16)

    @pl.when(jnp.logical_and(n > 0, jp == nj - 1))
    def _():
        r = DN_ALPHA * x1_ref[...] + g2_ref[0] * acc_scr[...]
        o_ref[...] = _layer_norm(r, lg_ref[...], lb_ref[...])


def _peer(h2, sel, u, v, x1, mod3, boff, seq, ln_g, ln_b, tm, te):
    t, d = h2.shape
    rows = te // N_KEYS
    bpb = seq // tm
    nj = N_EXPERTS // te
    a1, lrow, a2, r2 = sel
    ni = t // tm
    last = ni * nj - 1
    once = dict(pipeline_mode=pl.Buffered(1))
    cur = lambda n: jnp.minimum(n, last)
    prv = lambda n: jnp.maximum(n - 1, 0)
    full = pl.BlockSpec((PEER_HEADS, N_KEYS, tm), lambda n: (0, 0, cur(n) // nj), **once)
    part = pl.BlockSpec((PEER_HEADS, rows, tm), lambda n: (0, cur(n) % nj, cur(n) // nj))
    return pl.pallas_call(
        functools.partial(_peer_kernel, rows=rows, nj=nj),
        grid=(ni * nj + 1,),
        in_specs=[pl.BlockSpec((tm, d), lambda n: (cur(n) // nj, 0), **once), part, part, full, full,
                  pl.BlockSpec((te, d), lambda n: (cur(n) % nj, 0)),
                  pl.BlockSpec((te, d), lambda n: (prv(n) % nj, 0)),
                  pl.BlockSpec((tm, d), lambda n: (prv(n) // nj, 0), **once),
                  pl.BlockSpec((1, 1, d), lambda n: (boff + prv(n) // nj // bpb, 0, 5)),
                  _const_spec((1, d)), _const_spec((1, d))],
        out_specs=pl.BlockSpec((tm, d), lambda n: (prv(n) // nj, 0)),
        out_shape=jax.ShapeDtypeStruct((t, d), F32),
        scratch_shapes=[pltpu.VMEM((tm, d), F32), pltpu.VMEM((tm, te), BF16),
                        pltpu.VMEM((te, tm), F32), pltpu.VMEM((te, tm), F32)],
        compiler_params=_params(("arbitrary",)),
        name="peer_experts",
    )(h2, a1, lrow, a2, r2, u, v, x1, mod3, ln_g.reshape(1, d), ln_b.reshape(1, d))


def _trunk(x, mod3, boff, w):
    nbatch, seq, d = x.shape
    x2d = x.reshape(nbatch * seq, d)
    qkv = _proj(x2d, mod3, boff, seq, w["w_qkv"], PROJ_TM, PROJ_TN_QKV, BF16, False)
    gates = _proj(x2d, mod3, boff, seq, w["w_gate"], PROJ_TM, PROJ_TN_GATE, BF16, True)
    ya = _window(qkv, w["sink"], nbatch, seq)
    t = nbatch * seq
    vt = qkv[:, SZ_QA + 2 * SZ_QB:SZ_QA + 2 * SZ_QB + SZ_VB].T.reshape(B_HEADS, 128, t)
    vt_aug = jnp.concatenate([vt, jnp.ones((B_HEADS, DIFF_AUG, t), BF16)], axis=1)
    yb = _diff(qkv, vt_aug, w["lam_q1"], w["lam_k1"], w["lam_q2"], w["lam_k2"], w["norm_g"], nbatch, seq,
               DIFF_TQ, DIFF_TK)
    x1, h2 = _merge(ya, yb, gates, x2d, mod3, boff, seq, w["w_up_a"], w["w_up_b"], w["w_o"],
                    w["ln1_g"], w["ln1_b"], MERGE_TM)
    sel = _peer_q(h2, w["peer_wq"], w["peer_sub"], PEER_SELECT_TM)
    y = _peer(h2, sel, w["peer_u"], w["peer_v"], x1, mod3, boff, seq, w["ln2_g"], w["ln2_b"], PEER_TM, PEER_TE)
    return y.reshape(nbatch, seq, d)


def kernel(x_prompt, x_sample, c_prompt, c_sample, w_ada, b_ada, w_in, sink_a, lam_q1, lam_k1, lam_q2, lam_k2, diff_norm_g, w_up_a, w_up_b, w_o, ln1_g, ln1_b, peer_wq, peer_subkeys, peer_u, peer_v, ln2_g, ln2_b):
    layer = 0
    nbp, nbs = x_prompt.shape[0], x_sample.shape[0]
    pad = (-(nbp + nbs)) % 8
    c_all = jnp.concatenate([c_prompt, c_sample, jnp.zeros((pad, D_MODEL), F32)], axis=0)
    mod = _ada(c_all, w_ada[layer], b_ada[layer])
    mod3 = mod.reshape(mod.shape[0], 1, 6 * D_MODEL)

    wi = w_in[layer]
    o_ka = SZ_QA
    o_va = o_ka + SZ_KA
    o_qb = o_va + SZ_KA
    o_kb = o_qb + SZ_QB
    o_vb = o_kb + SZ_QB
    o_g = o_vb + SZ_VB
    w_qkv = jnp.concatenate([wi[:, :o_ka], wi[:, o_qb:o_g], wi[:, o_ka:o_qb]], axis=1).astype(BF16)
    weights = dict(
        w_qkv=w_qkv, w_gate=wi[:, o_g:].astype(BF16), sink=sink_a[layer].astype(F32),
        lam_q1=lam_q1[layer], lam_k1=lam_k1[layer], lam_q2=lam_q2[layer], lam_k2=lam_k2[layer],
        norm_g=diff_norm_g[layer],
        w_up_a=w_up_a[layer].astype(BF16), w_up_b=w_up_b[layer].astype(BF16), w_o=w_o[layer].astype(BF16),
        ln1_g=ln1_g[layer], ln1_b=ln1_b[layer],
        peer_wq=peer_wq[layer].astype(BF16), peer_sub=peer_subkeys[layer].astype(BF16),
        peer_u=peer_u[layer].astype(BF16), peer_v=peer_v[layer].astype(BF16),
        ln2_g=ln2_g[layer], ln2_b=ln2_b[layer],
    )
    y_prompt = _trunk(x_prompt, mod3, 0, weights)
    y_sample = _trunk(x_sample, mod3, nbp, weights)
    return (y_prompt, y_sample)
```

```python
import functools
import math

import jax
import jax.numpy as jnp
from jax import lax
from jax.experimental import pallas as pl
from jax.experimental.pallas import tpu as pltpu

F32 = jnp.float32
BF16 = jnp.bfloat16

D_MODEL = 2048
HEAD_DIM = 128
A_Q_HEADS = 8
A_KV_HEADS = 2
A_GROUP = A_Q_HEADS // A_KV_HEADS
WINDOW = 128
B_HEADS = 8
B_QK_DIM = 64
PEER_HEADS = 8
N_KEYS = 128
N_EXPERTS = N_KEYS * N_KEYS
PEER_TOPK = 16
SZ_QA = A_Q_HEADS * HEAD_DIM
SZ_KA = A_KV_HEADS * HEAD_DIM
SZ_QB = B_HEADS * 2 * B_QK_DIM
SZ_VB = B_HEADS * 2 * B_QK_DIM
DEPTH = 1
DN_ALPHA = (2.0 * DEPTH) ** 0.25
LN_EPS = 1e-5
RMS_EPS = 1e-5
LAM_INIT = 0.8 - 0.6 * math.exp(-0.3 * 0)

VMEM_LIMIT_V7X = 56 * 1024 * 1024

NT_DIMS = (((1,), (1,)), ((), ()))
TN_DIMS = (((0,), (0,)), ((), ()))


LANES = 128
GATE_SUB = 32

PROJ_TM, PROJ_TN_QKV, PROJ_TN_GATE = 512, 1536, 2048
DIFF_TQ, DIFF_TK = 256, 1024
MERGE_TM = 256
PEER_SELECT_TM = 512
PEER_TM, PEER_TE = 512, 1024


def _params(sem):
    return pltpu.CompilerParams(dimension_semantics=sem, vmem_limit_bytes=VMEM_LIMIT_V7X)


def _const_spec(shape):
    nd = len(shape)
    return pl.BlockSpec(shape, lambda *_: (0,) * nd, pipeline_mode=pl.Buffered(1))


def _layer_norm(r, g, b):
    mu = jnp.mean(r, axis=-1, keepdims=True)
    c = r - mu
    var = jnp.mean(c * c, axis=-1, keepdims=True)
    return c * lax.rsqrt(var + LN_EPS) * g + b


def _ada_kernel(c_ref, w_ref, b_ref, o_ref):
    c = c_ref[...]
    a = (c * jax.nn.sigmoid(c)).astype(BF16)
    o_ref[...] = jnp.dot(a, w_ref[...].astype(BF16), preferred_element_type=F32) + b_ref[...]


def _ada(c_all, w_ada, b_ada):
    nb, d = c_all.shape
    n = w_ada.shape[1]
    tn = 1024
    return pl.pallas_call(
        _ada_kernel,
        grid=(n // tn,),
        in_specs=[pl.BlockSpec((nb, d), lambda j: (0, 0)),
                  pl.BlockSpec((d, tn), lambda j: (0, j)),
                  pl.BlockSpec((1, tn), lambda j: (0, j))],
        out_specs=pl.BlockSpec((nb, tn), lambda j: (0, j)),
        out_shape=jax.ShapeDtypeStruct((nb, n), F32),
        compiler_params=_params(("arbitrary",)),
        name="ada",
    )(c_all, w_ada, b_ada.reshape(1, n))


def _proj_kernel(x_ref, sc_ref, sh_ref, wq_ref, wg_ref, qkv_ref, gate_ref, h_scr, *, nq):
    j = pl.program_id(1)

    @pl.when(j == 0)
    def _():
        h = x_ref[...] * (1.0 + sc_ref[0]) + sh_ref[0]
        h_scr[...] = h.astype(BF16)

    @pl.when(j < nq)
    def _():
        qkv_ref[...] = jnp.dot(h_scr[...], wq_ref[...], preferred_element_type=F32).astype(qkv_ref.dtype)

    @pl.when(j >= nq)
    def _():
        acc = jnp.dot(h_scr[...], wg_ref[...], preferred_element_type=F32)
        gate_ref[...] = jax.nn.sigmoid(acc).astype(gate_ref.dtype)


def _proj(x2d, mod3, boff, seq, w_qkv, w_gate, tm, tn_qkv, tn_gate):
    t, d = x2d.shape
    nq, ng = w_qkv.shape[1] // tn_qkv, w_gate.shape[1] // tn_gate
    bpb = seq // tm
    qcol = lambda j: jnp.minimum(j, nq - 1)
    gcol = lambda j: jnp.maximum(j - nq, 0)
    return pl.pallas_call(
        functools.partial(_proj_kernel, nq=nq),
        grid=(t // tm, nq + ng),
        in_specs=[pl.BlockSpec((tm, d), lambda i, j: (i, 0)),
                  pl.BlockSpec((1, 1, d), lambda i, j: (boff + i // bpb, 0, 1)),
                  pl.BlockSpec((1, 1, d), lambda i, j: (boff + i // bpb, 0, 0)),
                  pl.BlockSpec((d, tn_qkv), lambda i, j: (0, qcol(j))),
                  pl.BlockSpec((d, tn_gate), lambda i, j: (0, gcol(j)))],
        out_specs=[pl.BlockSpec((tm, tn_qkv), lambda i, j: (i, qcol(j))),
                   pl.BlockSpec((tm, tn_gate), lambda i, j: (i, gcol(j)))],
        out_shape=[jax.ShapeDtypeStruct((t, w_qkv.shape[1]), BF16),
                   jax.ShapeDtypeStruct((t, w_gate.shape[1]), BF16)],
        scratch_shapes=[pltpu.VMEM((tm, d), BF16)],
        compiler_params=_params(("parallel", "arbitrary")),
        name="proj",
    )(x2d, mod3, mod3, w_qkv, w_gate)


def _window_kernel(sink_ref, q_ref, kp_ref, kc_ref, kn_ref, vp_ref, vc_ref, vn_ref, o_ref, *, seq):
    blk = WINDOW
    n = pl.program_id(1)
    krel = lax.broadcasted_iota(jnp.int32, (3 * blk, blk), 0) - blk
    qrel = lax.broadcasted_iota(jnp.int32, (3 * blk, blk), 1)
    dist_i = jnp.abs(krel - qrel)
    kabs = n * blk + krel
    valid = (dist_i <= WINDOW) & (kabs >= 0) & (kabs < seq)
    dist = dist_i.astype(F32)
    scale = HEAD_DIM ** -0.5
    for g in range(A_KV_HEADS):
        cs = slice(g * HEAD_DIM, (g + 1) * HEAD_DIM)
        k = jnp.concatenate([kp_ref[:, cs], kc_ref[:, cs], kn_ref[:, cs]], axis=0)
        v = jnp.concatenate([vp_ref[:, cs], vc_ref[:, cs], vn_ref[:, cs]], axis=0)
        for j in range(A_GROUP):
            h = g * A_GROUP + j
            slope = 2.0 ** (-8.0 * (h + 1) / A_Q_HEADS)
            hs = slice(h * HEAD_DIM, (h + 1) * HEAD_DIM)
            q = q_ref[:, hs]
            s = lax.dot_general(k, q, NT_DIMS, preferred_element_type=F32) * scale
            s = jnp.where(valid, s - slope * dist, -jnp.inf)
            sink = sink_ref[h]
            m = jnp.maximum(jnp.max(s, axis=0, keepdims=True), sink)
            p = jnp.exp(s - m)
            l = jnp.sum(p, axis=0, keepdims=True) + jnp.exp(sink - m)
            o_t = lax.dot_general(v, p.astype(BF16), TN_DIMS, preferred_element_type=F32)
            o_t = o_t * (1.0 / l)
            o_ref[:, hs] = o_t.T.astype(o_ref.dtype)


def _window(qkv, sink, nbatch, seq):
    blk = WINDOW
    nb = seq // blk
    t = nbatch * seq
    kcol = (SZ_QA + SZ_QB + 2 * SZ_VB) // (A_KV_HEADS * HEAD_DIM)
    vcol = kcol + 1

    def row(b, n, off):
        return b * nb + jnp.clip(n + off, 0, nb - 1)

    kv_specs = [pl.BlockSpec((blk, SZ_KA), functools.partial(lambda b, n, col, off: (row(b, n, off), col), col=col, off=off))
                for col in (kcol, vcol) for off in (-1, 0, 1)]
    return pl.pallas_call(
        functools.partial(_window_kernel, seq=seq),
        grid=(nbatch, nb),
        in_specs=[pl.BlockSpec(memory_space=pltpu.SMEM),
                  pl.BlockSpec((blk, SZ_QA), lambda b, n: (b * nb + n, 0))] + kv_specs,
        out_specs=pl.BlockSpec((blk, SZ_QA), lambda b, n: (b * nb + n, 0)),
        out_shape=jax.ShapeDtypeStruct((t, SZ_QA), BF16),
        compiler_params=_params(("parallel", "parallel")),
        name="window_attn",
    )(sink, qkv, qkv, qkv, qkv, qkv, qkv, qkv)


DIFF_AUG = 16
DIFF_SPLIT = 256
DIFF_QK_AHEAD = 1
DIFF_PV_BEHIND = 2
DIFF_ROWS = 128
DIFF_UNDERFLOW = 160.0
DIFF_NORM_SLACK = 1.05
DIFF_FIRST_HEADS = (0, 2, 3, 4, 5)
LOG2E = math.log2(math.e)
LOG2E_PARTS = (1.4453125, -0.00262451171875, 7.063150405883789e-06)


def _diff_step(q_bias_rows, group_shift, k_ref, kc_ref, vt_ref, qt_scr, m_scr, acc_scr, s_scr, p_scr, dist, *,
               tq, tk, on_diag, first_head=0):
    nhm = 2 * B_HEADS
    ns, npb = DIFF_QK_AHEAD + 1, DIFF_PV_BEHIND + 1
    ngrp = tk // DIFF_SPLIT
    tiles = DIFF_SPLIT // DIFF_ROWS

    def qk(hm):
        h = hm // 2
        k_aug = jnp.concatenate([k_ref[:, h * 128:(h + 1) * 128], kc_ref[...]], axis=1)
        s_scr[hm % ns] = jnp.dot(k_aug, qt_scr[hm], preferred_element_type=F32)

    def softmax(hm):
        h, slot, pslot = hm // 2, hm % ns, hm % npb
        slope = 2.0 ** (-8.0 * (h + 1) / B_HEADS)
        shifts = [group_shift(h, g) for g in range(ngrp)]
        alphas = []
        for c in range(tq // LANES):
            cs = slice(c * LANES, (c + 1) * LANES)
            m_blk = None
            for g in range(ngrp):
                mrun = None
                for rb in range(tiles):
                    rs = slice(g * DIFF_SPLIT + rb * DIFF_ROWS, g * DIFF_SPLIT + (rb + 1) * DIFF_ROWS)
                    s = s_scr[slot, rs, cs]
                    if on_diag:
                        s = s - (slope * LOG2E) * dist[rs, cs]
                        s_scr[slot, rs, cs] = s
                    mrun = s if mrun is None else jnp.maximum(mrun, s)
                mg = jnp.max(mrun, axis=0, keepdims=True)
                if shifts[g] is not None:
                    mg = mg + shifts[g]
                m_blk = mg if m_blk is None else jnp.maximum(m_blk, mg)
            m_old = m_scr[hm:hm + 1, cs]
            m_new = jnp.maximum(m_old, m_blk)
            alphas.append(jnp.exp2(m_old - m_new))
            for g in range(ngrp):
                m_g = m_new if shifts[g] is None else m_new - shifts[g]
                for rb in range(tiles):
                    rs = slice(g * DIFF_SPLIT + rb * DIFF_ROWS, g * DIFF_SPLIT + (rb + 1) * DIFF_ROWS)
                    p_scr[pslot, rs, cs] = jnp.exp2(s_scr[slot, rs, cs] - m_g).astype(BF16)
            m_scr[hm:hm + 1, cs] = m_new
        return jnp.concatenate(alphas, axis=1)

    def pv(hm, alpha):
        r = jnp.dot(vt_ref[hm // 2], p_scr[hm % npb], preferred_element_type=F32)
        acc_scr[hm] = alpha * acc_scr[hm] + r

    lo = 2 * first_head
    for hm in range(lo, nhm):
        qt_scr[hm, 128:128 + DIFF_AUG, :] = q_bias_rows(hm // 2)
    for hm in range(lo, lo + DIFF_QK_AHEAD):
        qk(hm)
    alpha = {}
    for t in range(lo, nhm + DIFF_PV_BEHIND):
        if t + DIFF_QK_AHEAD < nhm:
            qk(t + DIFF_QK_AHEAD)
        if t < nhm:
            alpha[t] = softmax(t)
        if t - DIFF_PV_BEHIND >= lo:
            pv(t - DIFF_PV_BEHIND, alpha.pop(t - DIFF_PV_BEHIND))


def _diff_kernel(qn_ref, kn_ref, lq1_ref, lk1_ref, lq2_ref, lk2_ref, ng_ref, q_ref, k_ref, kc_ref, vt_ref, o_ref,
                 qt_scr, m_scr, acc_scr, s_scr, p_scr, dist_scr, *, tq, tk):
    b = pl.program_id(0)
    qi = pl.program_id(1)
    kv = pl.program_id(2)
    nkv = pl.num_programs(2)
    nq = pl.num_programs(1)
    kd = (qi * tq) // tk
    kb = (kd + kv) % nkv

    @pl.when(kv == 0)
    def _():
        m_scr[...] = jnp.full(m_scr.shape, -jnp.inf, F32)
        acc_scr[...] = jnp.zeros(acc_scr.shape, F32)
        row = lax.broadcasted_iota(jnp.int32, (2 * B_QK_DIM, tq), 0)
        zeros = jnp.zeros((128, tq), BF16)
        for h in range(B_HEADS):
            q_t = (q_ref[:, h * 128:(h + 1) * 128].astype(F32) * (B_QK_DIM ** -0.5 * LOG2E)).T
            qt_scr[2 * h, 0:128, :] = jnp.where(row < B_QK_DIM, q_t, 0.0).astype(BF16)
            qt_scr[2 * h + 1, 0:128, :] = jnp.where(row >= B_QK_DIM, q_t, 0.0).astype(BF16)
            qt_scr[2 * h, 128:256, :] = zeros
            qt_scr[2 * h + 1, 128:256, :] = zeros

    q0 = qi * tq
    k0 = kb * tk
    left = k0 + tk <= q0
    right = k0 >= q0 + tq
    args = (k_ref, kc_ref, vt_ref, qt_scr, m_scr, acc_scr, s_scr, p_scr)

    @pl.when(jnp.logical_or(left, right))
    def _():
        sgn = jnp.where(left, 1.0, -1.0).astype(F32)
        off = (q0 - k0).astype(F32)
        rid = lax.broadcasted_iota(jnp.int32, (DIFF_AUG, tq), 0)
        il = (lax.broadcasted_iota(jnp.int32, (DIFF_AUG, tq), 1) - DIFF_SPLIT // 2).astype(F32)

        def rows(h):
            g = sgn * 2.0 ** (-8.0 * (h + 1) / B_HEADS)
            r = jnp.where(rid < 3, -g * il, 0.0)
            for part, c in enumerate(LOG2E_PARTS):
                r = jnp.where(rid == 3 + part, g * c, r)
            return r.astype(BF16)

        def shift(h, grp):
            return -(sgn * 2.0 ** (-8.0 * (h + 1) / B_HEADS) * LOG2E) * (off - float(DIFF_SPLIT * grp))

        dmin = jnp.where(left, q0 - (k0 + tk - 1), k0 - (q0 + tq - 1)).astype(F32)
        nhm = 2 * B_HEADS
        qoff = (b * nq + qi) * nhm
        koff = (b * nkv + kb) * nhm
        doff = (b * nkv + kd) * nhm
        scale = DIFF_NORM_SLACK * (B_QK_DIM ** -0.5) * LOG2E
        lead = jnp.int32(0)
        ok = jnp.bool_(True)
        for h in range(B_HEADS):
            bound = jnp.float32(0.0)
            for mp in range(2):
                hm = 2 * h + mp
                bound = jnp.maximum(bound, qn_ref[qoff + hm] * (kn_ref[koff + hm] + kn_ref[doff + hm]))
            slope = 2.0 ** (-8.0 * (h + 1) / B_HEADS)
            ok = jnp.logical_and(ok, scale * bound - slope * LOG2E * dmin < -DIFF_UNDERFLOW)
            lead = lead + ok.astype(jnp.int32)
        for idx, first in enumerate(DIFF_FIRST_HEADS):
            upper = DIFF_FIRST_HEADS[idx + 1] if idx + 1 < len(DIFF_FIRST_HEADS) else B_HEADS + 1

            @pl.when(jnp.logical_and(lead >= first, lead < upper))
            def _(first=first):
                _diff_step(rows, shift, *args, None, tq=tq, tk=tk, on_diag=False, first_head=first)

    @pl.when(jnp.logical_not(jnp.logical_or(left, right)))
    def _():
        kpos = k0 + lax.broadcasted_iota(jnp.int32, (tk, tq), 0)
        qpos = q0 + lax.broadcasted_iota(jnp.int32, (tk, tq), 1)
        dist_scr[...] = jnp.abs(kpos - qpos).astype(F32)
        _diff_step(lambda h: jnp.zeros((DIFF_AUG, tq), BF16), lambda h, g: None, *args, dist_scr,
                   tq=tq, tk=tk, on_diag=True)

    @pl.when(kv == nkv - 1)
    def _():
        lam = (jnp.exp(jnp.sum(lq1_ref[...] * lk1_ref[...], axis=-1, keepdims=True))
               - jnp.exp(jnp.sum(lq2_ref[...] * lk2_ref[...], axis=-1, keepdims=True)) + LAM_INIT)
        for h in range(B_HEADS):
            a0, a1 = acc_scr[2 * h], acc_scr[2 * h + 1]
            o0 = a0[0:128] * (1.0 / a0[128:129])
            o1 = a1[0:128] * (1.0 / a1[128:129])
            o = (o0 - lam * o1).T
            o = o * lax.rsqrt(jnp.mean(o * o, axis=-1, keepdims=True) + RMS_EPS) * ng_ref[...]
            o_ref[:, h * 128:(h + 1) * 128] = (o * (1.0 - LAM_INIT)).astype(o_ref.dtype)


def _rownorm_kernel(x_ref, ind_ref, o_ref):
    x = x_ref[...].astype(F32)
    ss = jnp.dot(x * x, ind_ref[...], preferred_element_type=F32)
    o_ref[0] = jnp.broadcast_to(jnp.sqrt(jnp.max(ss, axis=0, keepdims=True)), o_ref.shape[1:])


def _rownorm_max(qkv, col, rows):
    t = qkv.shape[0]
    ind = (jnp.arange(SZ_QB)[:, None] // B_QK_DIM == jnp.arange(LANES)[None, :]).astype(F32)
    out = pl.pallas_call(
        _rownorm_kernel,
        grid=(t // rows,),
        in_specs=[pl.BlockSpec((rows, SZ_QB), lambda i: (i, col)), _const_spec(ind.shape)],
        out_specs=pl.BlockSpec((1, 8, LANES), lambda i: (i, 0, 0)),
        out_shape=jax.ShapeDtypeStruct((t // rows, 8, LANES), F32),
        compiler_params=_params(("parallel",)),
        name="diff_rownorm",
    )(qkv, ind)
    return out[:, 0, :2 * B_HEADS].reshape(-1)


def _diff(qkv, vt_aug, lam_q1, lam_k1, lam_q2, lam_k2, norm_g, nbatch, seq, tq, tk):
    t = nbatch * seq
    nq, nk = seq // tq, seq // tk
    qn = _rownorm_max(qkv, 1, tq)
    kn = _rownorm_max(qkv, 2, tk)
    kblk = lambda i, j: ((i * tq) // tk + j) % nk
    w = SZ_QB
    nhm = 2 * B_HEADS
    small = lambda a: a.reshape(1, -1).astype(F32)
    sspec = lambda n: pl.BlockSpec((1, n), lambda b, i, j: (0, 0))
    assert tq == DIFF_SPLIT and tk % DIFF_SPLIT == 0
    jl = (jnp.arange(tk) % DIFF_SPLIT - DIFF_SPLIT // 2).astype(F32)
    kc = jnp.zeros((tk, 128), F32).at[:, 0:3].set(jnp.asarray(LOG2E_PARTS, F32)).at[:, 3:6].set(jl[:, None])
    return pl.pallas_call(
        functools.partial(_diff_kernel, tq=tq, tk=tk),
        grid=(nbatch, nq, nk),
        in_specs=[pl.BlockSpec(memory_space=pltpu.SMEM)] * 2 + [sspec(B_QK_DIM)] * 4 + [sspec(2 * B_QK_DIM),
                  pl.BlockSpec((tq, w), lambda b, i, j: (b * nq + i, 1)),
                  pl.BlockSpec((tk, w), lambda b, i, j: (b * nk + kblk(i, j), 2)),
                  _const_spec((tk, 128)),
                  pl.BlockSpec((B_HEADS, 128 + DIFF_AUG, tk), lambda b, i, j: (0, 0, b * nk + kblk(i, j)))],
        out_specs=pl.BlockSpec((tq, w), lambda b, i, j: (b * nq + i, 0)),
        out_shape=jax.ShapeDtypeStruct((t, w), BF16),
        scratch_shapes=[pltpu.VMEM((nhm, 256, tq), BF16),
                        pltpu.VMEM((nhm, tq), F32),
                        pltpu.VMEM((nhm, 128 + DIFF_AUG, tq), F32),
                        pltpu.VMEM((DIFF_QK_AHEAD + 1, tk, tq), F32),
                        pltpu.VMEM((DIFF_PV_BEHIND + 1, tk, tq), BF16),
                        pltpu.VMEM((tk, tq), F32)],
        compiler_params=_params(("parallel", "parallel", "arbitrary")),
        name="diff_attn",
    )(qn, kn, small(lam_q1), small(lam_k1), small(lam_q2), small(lam_k2), small(norm_g), qkv, qkv, kc.astype(BF16), vt_aug)


def _merge_kernel(ya_ref, yb_ref, ga_ref, gb_ref, x_ref, g1_ref, sc2_ref, sh2_ref,
                  wua_ref, wub_ref, wo_ref, lg_ref, lb_ref, x1_ref, h2_ref):
    ua = jnp.dot(ya_ref[...], wua_ref[...], preferred_element_type=F32)
    ub = jnp.dot(yb_ref[...], wub_ref[...], preferred_element_type=F32)
    merged = ga_ref[...] * ua + gb_ref[...] * ub
    z = jnp.dot(merged.astype(BF16), wo_ref[...], preferred_element_type=F32)
    r = DN_ALPHA * x_ref[...] + g1_ref[0] * z
    x1 = _layer_norm(r, lg_ref[...], lb_ref[...])
    x1_ref[...] = x1
    h2_ref[...] = (x1 * (1.0 + sc2_ref[0]) + sh2_ref[0]).astype(BF16)


def _merge(ya, yb, gates, x2d, mod3, boff, seq, wua, wub, wo, ln_g, ln_b, tm):
    t, d = x2d.shape
    bpb = seq // tm
    mspec = lambda c: pl.BlockSpec((1, 1, d), lambda i: (boff + i // bpb, 0, c))
    row = lambda n, c=0: pl.BlockSpec((tm, n), lambda i: (i, c))
    return pl.pallas_call(
        _merge_kernel,
        grid=(t // tm,),
        in_specs=[row(SZ_QA), row(SZ_VB), row(d, 0), row(d, 1), row(d),
                  mspec(2), mspec(4), mspec(3),
                  _const_spec(wua.shape), _const_spec(wub.shape), _const_spec(wo.shape),
                  _const_spec((1, d)), _const_spec((1, d))],
        out_specs=[row(d), row(d)],
        out_shape=[jax.ShapeDtypeStruct((t, d), F32), jax.ShapeDtypeStruct((t, d), BF16)],
        compiler_params=_params(("parallel",)),
        name="merge_ln1",
    )(ya, yb, gates, gates, x2d, mod3, mod3, mod3, wua, wub, wo, ln_g.reshape(1, d), ln_b.reshape(1, d))


def _top16(s, ties):
    n, t = s.shape
    iota = lax.broadcasted_iota(jnp.int32, (n, t), 0)
    rank = jnp.full((n, t), float(PEER_TOPK), F32)
    work = s
    vals = []
    for a in range(PEER_TOPK):
        m = jnp.max(work, axis=0, keepdims=True)
        sel = work == m
        if ties:
            idx = jnp.min(jnp.where(sel, iota, n), axis=0, keepdims=True)
            sel = iota == idx
        rank = jnp.where(sel, float(a), rank)
        work = jnp.where(sel, -jnp.inf, work)
        vals.append(m)
    bad = None
    if not ties:
        removed = jnp.sum(jnp.where(rank < float(PEER_TOPK), 1.0, 0.0), axis=0, keepdims=True)
        bad = jnp.where(removed != float(PEER_TOPK), 1.0, 0.0)
    return jnp.concatenate(vals, axis=0), rank, bad


def _peer_select(s1, s2, ties):
    v1, r1, bad1 = _top16(s1, ties)
    v2, r2, bad2 = _top16(s2, ties)
    half = PEER_TOPK // 2
    cand = jnp.concatenate([v1[0:1] + v2] + [v1[a:a + 1] + v2[0:half] for a in range(1, half)]
                           + [v1[half:] + v2[0:1]], axis=0)
    cv, crank, bad3 = _top16(cand, ties)
    chosen = jnp.where(crank < float(PEER_TOPK), 1.0, 0.0)
    z = jnp.sum(jnp.exp(cv - cv[0:1, :]), axis=0, keepdims=True)
    inv_z = 0.5 / z
    tail = PEER_TOPK + (half - 1) * half
    cnts = [jnp.sum(chosen[0:PEER_TOPK], axis=0, keepdims=True)]
    cnts += [jnp.sum(chosen[PEER_TOPK + (a - 1) * half:PEER_TOPK + a * half], axis=0, keepdims=True)
             for a in range(1, half)]
    cnts += [chosen[tail + a:tail + a + 1] for a in range(half)]
    lrow = jnp.zeros_like(s1)
    for a in range(PEER_TOPK):
        lrow = jnp.where(r1 == float(a), cnts[a], lrow)
    a1 = jnp.where(r1 < float(PEER_TOPK), jnp.exp(s1 - v1[0:1, :]) * inv_z, 0.0)
    a2 = jnp.exp(s2 - v2[0:1, :])
    bad = None if ties else bad1 + bad2 + bad3
    return (a1, lrow, a2, r2), bad


def _peer_q_kernel(h2_ref, wq_ref, sub_ref, a1_ref, lrow_ref, a2_ref, r2_ref, q_scr):
    hd = pl.program_id(1)
    kd = N_KEYS
    out_refs = (a1_ref, lrow_ref, a2_ref, r2_ref)

    @pl.when(hd == 0)
    def _():
        q = jnp.dot(h2_ref[...], wq_ref[...], preferred_element_type=F32).astype(BF16)
        for h in range(PEER_HEADS):
            q_scr[h] = q[:, h * 2 * kd:(h + 1) * 2 * kd]

    q = q_scr[hd]
    s1 = lax.dot_general(sub_ref[0, 0], q[:, :kd], NT_DIMS, preferred_element_type=F32)
    s2 = lax.dot_general(sub_ref[0, 1], q[:, kd:], NT_DIMS, preferred_element_type=F32)
    outs, bad = _peer_select(s1, s2, ties=False)
    for ref, val in zip(out_refs, outs):
        ref[0] = val

    @pl.when(jnp.sum(bad) > 0.0)
    def _():
        for c in range(s1.shape[1] // LANES):
            cs = slice(c * LANES, (c + 1) * LANES)

            @pl.when(jnp.sum(bad[:, cs]) > 0.0)
            def _(cs=cs):
                redo, _ = _peer_select(s1[:, cs], s2[:, cs], ties=True)
                for ref, val in zip(out_refs, redo):
                    ref[0, :, cs] = val


def _peer_q(h2, wq, sub, tm):
    t, d = h2.shape
    out = jax.ShapeDtypeStruct((PEER_HEADS, N_KEYS, t), F32)
    ospec = pl.BlockSpec((1, N_KEYS, tm), lambda i, h: (h, 0, i))
    return pl.pallas_call(
        _peer_q_kernel,
        grid=(t // tm, PEER_HEADS),
        in_specs=[pl.BlockSpec((tm, d), lambda i, h: (i, 0)),
                  _const_spec(wq.shape),
                  pl.BlockSpec((1, 2, N_KEYS, N_KEYS), lambda i, h: (h, 0, 0, 0))],
        out_specs=[ospec] * 4,
        out_shape=[out] * 4,
        scratch_shapes=[pltpu.VMEM((PEER_HEADS, tm, 2 * N_KEYS), BF16)],
        compiler_params=_params(("parallel", "arbitrary")),
        name="peer_select",
    )(h2, wq, sub)


def _peer_kernel(h2_ref, a1_ref, lrow_ref, a2_ref, r2_ref, u_ref, v_ref, x1_ref, g2_ref, lg_ref, lb_ref,
                 o_ref, acc_scr, w_scr, xu_scr, g_scr, *, rows, nj):
    n = pl.program_id(0)
    jp = jnp.maximum(n - 1, 0) % nj

    @pl.when(n == 0)
    def _():
        w_scr[...] = jnp.zeros(w_scr.shape, BF16)

    @pl.when(jp == 0)
    def _():
        acc_scr[...] = jnp.zeros(acc_scr.shape, F32)

    acc_scr[...] += jnp.dot(w_scr[...], v_ref[...], preferred_element_type=F32)

    tm = h2_ref.shape[0]
    for tc in range(tm // LANES):
        cs = slice(tc * LANES, (tc + 1) * LANES)
        for sb in range(N_KEYS // GATE_SUB):
            ss = slice(sb * GATE_SUB, (sb + 1) * GATE_SUB)
            g = [None] * rows
            for h in range(PEER_HEADS):
                r2 = r2_ref[h, ss, cs]
                a2 = a2_ref[h, ss, cs]
                for r in range(rows):
                    term = jnp.where(r2 < lrow_ref[h, r:r + 1, cs], a1_ref[h, r:r + 1, cs] * a2, 0.0)
                    g[r] = term if g[r] is None else g[r] + term
            for r in range(rows):
                g_scr[r * N_KEYS + sb * GATE_SUB:r * N_KEYS + (sb + 1) * GATE_SUB, cs] = g[r]

    xu_scr[...] = lax.dot_general(u_ref[...], h2_ref[...], NT_DIMS, preferred_element_type=F32)
    for r in range(rows):
        rs = slice(r * N_KEYS, (r + 1) * N_KEYS)
        for tc in range(tm // LANES):
            cs = slice(tc * LANES, (tc + 1) * LANES)
            xu = xu_scr[rs, cs]
            w_t = xu * (1.0 + lax.erf(xu * (2.0 ** -0.5))) * g_scr[rs, cs]
            w_scr[cs, rs] = w_t.T.astype(BF16)

    @pl.when(jnp.logical_and(n > 0, jp == nj - 1))
    def _():
        r = DN_ALPHA * x1_ref[...] + g2_ref[0] * acc_scr[...]
        o_ref[...] = _layer_norm(r, lg_ref[...], lb_ref[...])


def _peer(h2, sel, u, v, x1, mod3, boff, seq, ln_g, ln_b, tm, te):
    t, d = h2.shape
    rows = te // N_KEYS
    bpb = seq // tm
    nj = N_EXPERTS // te
    a1, lrow, a2, r2 = sel
    ni = t // tm
    last = ni * nj - 1
    once = dict(pipeline_mode=pl.Buffered(1))
    cur = lambda n: jnp.minimum(n, last)
    prv = lambda n: jnp.maximum(n - 1, 0)
    full = pl.BlockSpec((PEER_HEADS, N_KEYS, tm), lambda n: (0, 0, cur(n) // nj), **once)
    part = pl.BlockSpec((PEER_HEADS, rows, tm), lambda n: (0, cur(n) % nj, cur(n) // nj))
    return pl.pallas_call(
        functools.partial(_peer_kernel, rows=rows, nj=nj),
        grid=(ni * nj + 1,),
        in_specs=[pl.BlockSpec((tm, d), lambda n: (cur(n) // nj, 0), **once), part, part, full, full,
                  pl.BlockSpec((te, d), lambda n: (cur(n) % nj, 0)),
                  pl.BlockSpec((te, d), lambda n: (prv(n) % nj, 0)),
                  pl.BlockSpec((tm, d), lambda n: (prv(n) // nj, 0), **once),
                  pl.BlockSpec((1, 1, d), lambda n: (boff + prv(n) // nj // bpb, 0, 5)),
                  _const_spec((1, d)), _const_spec((1, d))],
        out_specs=pl.BlockSpec((tm, d), lambda n: (prv(n) // nj, 0)),
        out_shape=jax.ShapeDtypeStruct((t, d), F32),
        scratch_shapes=[pltpu.VMEM((tm, d), F32), pltpu.VMEM((tm, te), BF16),
                        pltpu.VMEM((te, tm), F32), pltpu.VMEM((te, tm), F32)],
        compiler_params=_params(("arbitrary",)),
        name="peer_experts",
    )(h2, a1, lrow, a2, r2, u, v, x1, mod3, ln_g.reshape(1, d), ln_b.reshape(1, d))


def _trunk(x, mod3, boff, w):
    nbatch, seq, d = x.shape
    x2d = x.reshape(nbatch * seq, d)
    qkv, gates = _proj(x2d, mod3, boff, seq, w["w_qkv"], w["w_gate"], PROJ_TM, PROJ_TN_QKV, PROJ_TN_GATE)
    ya = _window(qkv, w["sink"], nbatch, seq)
    t = nbatch * seq
    vt = qkv[:, SZ_QA + 2 * SZ_QB:SZ_QA + 2 * SZ_QB + SZ_VB].T.reshape(B_HEADS, 128, t)
    vt_aug = jnp.concatenate([vt, jnp.ones((B_HEADS, DIFF_AUG, t), BF16)], axis=1)
    yb = _diff(qkv, vt_aug, w["lam_q1"], w["lam_k1"], w["lam_q2"], w["lam_k2"], w["norm_g"], nbatch, seq,
               DIFF_TQ, DIFF_TK)
    x1, h2 = _merge(ya, yb, gates, x2d, mod3, boff, seq, w["w_up_a"], w["w_up_b"], w["w_o"],
                    w["ln1_g"], w["ln1_b"], MERGE_TM)
    sel = _peer_q(h2, w["peer_wq"], w["peer_sub"], PEER_SELECT_TM)
    y = _peer(h2, sel, w["peer_u"], w["peer_v"], x1, mod3, boff, seq, w["ln2_g"], w["ln2_b"], PEER_TM, PEER_TE)
    return y.reshape(nbatch, seq, d)


def kernel(x_prompt, x_sample, c_prompt, c_sample, w_ada, b_ada, w_in, sink_a, lam_q1, lam_k1, lam_q2, lam_k2, diff_norm_g, w_up_a, w_up_b, w_o, ln1_g, ln1_b, peer_wq, peer_subkeys, peer_u, peer_v, ln2_g, ln2_b):
    layer = 0
    nbp, nbs = x_prompt.shape[0], x_sample.shape[0]
    pad = (-(nbp + nbs)) % 8
    c_all = jnp.concatenate([c_prompt, c_sample, jnp.zeros((pad, D_MODEL), F32)], axis=0)
    mod = _ada(c_all, w_ada[layer], b_ada[layer])
    mod3 = mod.reshape(mod.shape[0], 1, 6 * D_MODEL)

    wi = w_in[layer]
    o_ka = SZ_QA
    o_va = o_ka + SZ_KA
    o_qb = o_va + SZ_KA
    o_kb = o_qb + SZ_QB
    o_vb = o_kb + SZ_QB
    o_g = o_vb + SZ_VB
    w_qkv = jnp.concatenate([wi[:, :o_ka], wi[:, o_qb:o_g], wi[:, o_ka:o_qb]], axis=1).astype(BF16)
    weights = dict(
        w_qkv=w_qkv, w_gate=wi[:, o_g:].astype(BF16), sink=sink_a[layer].astype(F32),
        lam_q1=lam_q1[layer], lam_k1=lam_k1[layer], lam_q2=lam_q2[layer], lam_k2=lam_k2[layer],
        norm_g=diff_norm_g[layer],
        w_up_a=w_up_a[layer].astype(BF16), w_up_b=w_up_b[layer].astype(BF16), w_o=w_o[layer].astype(BF16),
        ln1_g=ln1_g[layer], ln1_b=ln1_b[layer],
        peer_wq=peer_wq[layer].astype(BF16), peer_sub=peer_subkeys[layer].astype(BF16),
        peer_u=peer_u[layer].astype(BF16), peer_v=peer_v[layer].astype(BF16),
        ln2_g=ln2_g[layer], ln2_b=ln2_b[layer],
    )
    y_prompt = _trunk(x_prompt, mod3, 0, weights)
    y_sample = _trunk(x_sample, mod3, nbp, weights)
    return (y_prompt, y_sample)
```
